```python
import jax, jax.numpy as jnp
from jax import lax
import numpy as np

D_MODEL = 1024
BATCH = 16
SEQ = 2048
DEPTH = 4
DEC_BATCH = 128
DEC_SEQ = 8
PAST_LEN = 8192
PAGE_SIZE = 128

HEAD_DIM = 64
FOX_HEADS = 8
FOX_KV_HEADS = 2
MLA_HEADS = 8
MLA_Q_LORA = 256
MLA_KV_LORA = 128
MLA_NOPE = 64
MLA_ROPE = 32
MLA_V = 64
MLA_THETA = 10000.0
NSA_HEADS = 16
NSA_KV_HEADS = 2
CMP_L = 32
CMP_D = 16
SEL_L = 64
TOP_N = 16
WINDOW = 512
ROPE_THETA = 500000.0
ROPE_DIM = HEAD_DIM // 4
FF_DIM = 2816
Q_BLOCK = 128
EPS = 1e-6
FORGET_BIAS = 3.0
FORCE_SCORE = 1e9
NEG_INF = -1e30
N_EVEN = (DEPTH + 1) // 2
N_ODD = DEPTH // 2
EV_SPLITS = (FOX_HEADS * HEAD_DIM, FOX_KV_HEADS * HEAD_DIM, FOX_KV_HEADS * HEAD_DIM, FOX_HEADS, MLA_Q_LORA, MLA_KV_LORA, MLA_ROPE)
EV_IN = FOX_HEADS * HEAD_DIM + 2 * FOX_KV_HEADS * HEAD_DIM + FOX_HEADS + MLA_Q_LORA + MLA_KV_LORA + MLA_ROPE
EV_OUT = FOX_HEADS * HEAD_DIM + MLA_HEADS * MLA_V
OD_SPLITS = (NSA_HEADS * HEAD_DIM,) + (NSA_KV_HEADS * HEAD_DIM,) * 6 + (3 * NSA_HEADS,)
OD_IN = NSA_HEADS * HEAD_DIM + 6 * NSA_KV_HEADS * HEAD_DIM + 3 * NSA_HEADS
OD_OUT = NSA_HEADS * HEAD_DIM

kernel_name = 'fox_mla_nsa_macaron_decoder_step'


def _rms(x, g):
    xf = x.astype(jnp.float32)
    y = xf * lax.rsqrt(jnp.mean(xf * xf, axis=-1, keepdims=True) + EPS)
    return (y * g.astype(jnp.float32)).astype(x.dtype)


def _swiglu(x, wg, wu, wd):
    return (jax.nn.silu(x @ wg) * (x @ wu)) @ wd


def _split(z, sizes):
    out, o = [], 0
    for s in sizes:
        out.append(z[..., o:o + s])
        o += s
    return out


def _rope(x, pos, theta, rot_dim):
    half = rot_dim // 2
    inv = jnp.float32(theta) ** (-jnp.arange(half, dtype=jnp.float32) / half)
    ang = pos.astype(jnp.float32)[:, None] * inv[None, :]
    cos = jnp.cos(ang)[None, :, None, :]
    sin = jnp.sin(ang)[None, :, None, :]
    xf = x.astype(jnp.float32)
    x1, x2 = xf[..., :half], xf[..., half:rot_dim]
    return jnp.concatenate([x1 * cos - x2 * sin, x1 * sin + x2 * cos, xf[..., rot_dim:]], axis=-1).astype(x.dtype)


def _paged(pool, layer, page_table):
    g = pool[layer, page_table]
    return g.reshape((g.shape[0], g.shape[1] * g.shape[2]) + g.shape[3:])


def _sweep(block_fn, T):
    if T <= Q_BLOCK:
        return block_fn(0, T)
    o = lax.map(lambda i: block_fn(i * Q_BLOCK, Q_BLOCK), jnp.arange(T // Q_BLOCK))
    return jnp.moveaxis(o, 0, 1).reshape((o.shape[1], T) + o.shape[3:])


def _attend(q, k, v, q_pos, k_pos, scale, cq, ck):
    B, Tq, H, dq = q.shape
    L, G = k.shape[1], k.shape[2]
    R = H // G
    qg = q.reshape(B, Tq, G, R, dq)
    s = jnp.einsum('btgrd,blgd->bgrtl', qg, k).astype(jnp.float32) * scale
    if cq is not None:
        cqg = jnp.moveaxis(cq.astype(jnp.float32).reshape(B, Tq, G, R), 1, 3)[..., None]
        ckg = jnp.moveaxis(ck.astype(jnp.float32).reshape(B, L, G, R), 1, 3)[..., None, :]
        s = s + (cqg - ckg)
    mask = k_pos[None, :] <= q_pos[:, None]
    p = jax.nn.softmax(jnp.where(mask, s, NEG_INF), axis=-1)
    o = jnp.einsum('bgrtl,blgd->btgrd', p.astype(v.dtype), v)
    return o.reshape(B, Tq, H, v.shape[-1])


def _dense_causal(q, k, v, q_pos, k_pos, scale, cq=None, ck=None):
    def blk(st, n):
        sl = lambda a: lax.dynamic_slice_in_dim(a, st, n, 1)
        return _attend(sl(q), k, v, lax.dynamic_slice_in_dim(q_pos, st, n, 0), k_pos, scale,
                       None if cq is None else sl(cq), ck)
    return _sweep(blk, q.shape[1])


def _even_mixer(h, pos, past, w_in, b_f, q_norm, w_q_up, kv_norm, w_uk, w_uv, w_out):
    B, T, _ = h.shape
    q_f, k_f, v_f, f_lg, c_q, c_kv, k_pe = _split(h @ w_in, EV_SPLITS)
    q_f = q_f.reshape(B, T, FOX_HEADS, HEAD_DIM)
    kv_new = jnp.stack([k_f.reshape(B, T, FOX_KV_HEADS, HEAD_DIM), v_f.reshape(B, T, FOX_KV_HEADS, HEAD_DIM)], axis=2)
    logf_new = jax.nn.log_sigmoid((f_lg + b_f).astype(jnp.float32))
    qm = (_rms(c_q, q_norm) @ w_q_up).reshape(B, T, MLA_HEADS, MLA_NOPE + MLA_ROPE)
    q_nope = qm[..., :MLA_NOPE]
    q_pe = _rope(qm[..., MLA_NOPE:], pos, MLA_THETA, MLA_ROPE)
    k_pe = _rope(k_pe[:, :, None, :], pos, MLA_THETA, MLA_ROPE)[:, :, 0]
    lat_new = jnp.concatenate([_rms(c_kv, kv_norm), k_pe], axis=-1)
    if past is None:
        kv_all, logf_all, lat_all, k_pos = kv_new, logf_new, lat_new, pos
    else:
        kv_p, logf_p, lat_p = past
        kv_all = jnp.concatenate([kv_p, kv_new], axis=1)
        logf_all = jnp.concatenate([logf_p.astype(jnp.float32), logf_new], axis=1)
        lat_all = jnp.concatenate([lat_p, lat_new], axis=1)
        k_pos = jnp.arange(kv_all.shape[1])
    csum = jnp.cumsum(logf_all, axis=1)
    o_fox = _dense_causal(q_f, kv_all[:, :, 0], kv_all[:, :, 1], pos, k_pos, HEAD_DIM ** -0.5,
                          csum[:, csum.shape[1] - T:], csum)
    q_lat = jnp.einsum('bthn,chn->bthc', q_nope, w_uk)
    o_lat = _dense_causal(jnp.concatenate([q_lat, q_pe], axis=-1), lat_all[:, :, None, :],
                          lat_all[:, :, None, :MLA_KV_LORA], pos, k_pos, (MLA_NOPE + MLA_ROPE) ** -0.5)
    o_mla = jnp.einsum('bthc,chv->bthv', o_lat, w_uv)
    out = jnp.concatenate([o_fox.reshape(B, T, -1), o_mla.reshape(B, T, -1)], axis=-1) @ w_out
    return out, kv_new, logf_new, lat_new


def _compress(k, pe, w1, w2):
    B, L, G, D = k.shape
    nch = L // CMP_D
    ch = k[:, :nch * CMP_D].reshape(B, nch, CMP_D, G, D)
    r = CMP_L // CMP_D
    nc = nch - r + 1
    blk = jnp.concatenate([ch[:, i:i + nc] for i in range(r)], axis=2)
    blk = blk + pe[None, None, :, None, :]
    flat = jnp.moveaxis(blk, 3, 2).reshape(B, nc, G, CMP_L * D)
    return jax.nn.gelu(flat @ w1) @ w2


def _nsa_block(q, q_pos, gates, kcmp, vcmp, ksb, vsb, kw, vw, kw_pos, n_sel):
    B, T, H, D = q.shape
    G = kcmp.shape[2]
    R = H // G
    scale = D ** -0.5
    qg = q.reshape(B, T, G, R, D)
    nc = kcmp.shape[1]
    c_end = jnp.arange(nc) * CMP_D + CMP_L - 1
    mc = c_end[None, :] <= q_pos[:, None]
    sc = jnp.einsum('btgrd,bcgd->bgrtc', qg, kcmp).astype(jnp.float32) * scale
    pc = jax.nn.softmax(jnp.where(mc, sc, NEG_INF), axis=-1) * mc
    o_cmp = jnp.einsum('bgrtc,bcgd->btgrd', pc.astype(vcmp.dtype), vcmp)
    ns = ksb.shape[2]
    ratio, span = SEL_L // CMP_D, CMP_L // CMP_D
    pg = pc.sum(axis=2)
    pg = jnp.pad(pg, ((0, 0), (0, 0), (0, 0), (0, ns * ratio + ratio + span - 2 - nc)))
    imp = jnp.zeros(pg.shape[:-1] + (ns,), jnp.float32)
    for m in range(ratio):
        for n in range(span):
            imp = imp + pg[..., m + n:m + n + ns * ratio:ratio]
    blk = jnp.arange(ns)
    cur = q_pos // SEL_L
    forced = (blk[None, :] == 0) | (blk[None, :] == cur[:, None]) | (blk[None, :] == cur[:, None] - 1)
    valid = blk[None, :] * SEL_L <= q_pos[:, None]
    score = jnp.where(valid, jnp.where(forced, FORCE_SCORE, imp), -1.0)
    _, idx = lax.top_k(score, n_sel)
    take = jax.vmap(jax.vmap(lambda kk, ii: kk[ii]))
    kb = take(ksb, idx)
    vb = take(vsb, idx)
    tok = idx[..., None] * SEL_L + jnp.arange(SEL_L)
    ms = (tok <= q_pos[None, None, :, None, None])[:, :, None]
    ss = jnp.einsum('btgrd,bgtnld->bgrtnl', qg, kb).astype(jnp.float32) * scale
    ss = jnp.where(ms, ss, NEG_INF)
    ps = jax.nn.softmax(ss.reshape(B, G, R, T, -1), axis=-1).reshape(ss.shape)
    o_slc = jnp.einsum('bgrtnl,bgtnld->btgrd', ps.astype(vb.dtype), vb)
    dpos = q_pos[:, None] - kw_pos[None, :]
    mw = (dpos >= 0) & (dpos < WINDOW) & (kw_pos[None, :] >= 0)
    sw = jnp.einsum('btgrd,blgd->bgrtl', qg, kw).astype(jnp.float32) * scale
    pw = jax.nn.softmax(jnp.where(mw, sw, NEG_INF), axis=-1)
    o_win = jnp.einsum('bgrtl,blgd->btgrd', pw.astype(vw.dtype), vw)
    g = gates.reshape(B, T, G, R, 3).astype(q.dtype)
    o = g[..., 0:1] * o_cmp + g[..., 1:2] * o_slc + g[..., 2:3] * o_win
    return o.reshape(B, T, H, D)


def _odd_mixer(h, pos, past, w_in, b_gate, cmp_pe, cmp_w1, cmp_w2, w_out):
    B, T, _ = h.shape
    q, kc, vc, ks, vs, kw, vw, g = _split(h @ w_in, OD_SPLITS)
    kvh = lambda a: a.reshape(B, T, NSA_KV_HEADS, HEAD_DIM)
    rot = lambda a: _rope(a, pos, ROPE_THETA, ROPE_DIM)
    q = rot(q.reshape(B, T, NSA_HEADS, HEAD_DIM))
    cmp_new = jnp.stack([rot(kvh(kc)), kvh(vc)], axis=2)
    slc_new = jnp.stack([rot(kvh(ks)), kvh(vs)], axis=2)
    win_new = jnp.stack([rot(kvh(kw)), kvh(vw)], axis=2)
    gates = jax.nn.sigmoid((g + b_gate).astype(jnp.float32))
    gates = gates.reshape(B, T, NSA_HEADS, 3)
    if past is None:
        past_len = 0
        cmp_all, slc_all = cmp_new, slc_new
        win_all = jnp.pad(win_new, ((0, 0), (WINDOW, 0), (0, 0), (0, 0), (0, 0)))
        win_state = win_new[:, T - min(WINDOW, T):]
    else:
        cmp_p, slc_p, win_p = past
        past_len = cmp_p.shape[1]
        cmp_all = jnp.concatenate([cmp_p, cmp_new], axis=1)
        slc_all = jnp.concatenate([slc_p, slc_new], axis=1)
        win_cat = jnp.concatenate([win_p, win_new], axis=1)
        win_state = win_cat[:, T:]
        win_all = jnp.pad(win_cat, ((0, 0), (WINDOW - win_p.shape[1], 0), (0, 0), (0, 0), (0, 0)))
    win_pos = jnp.arange(win_all.shape[1]) + (past_len - WINDOW)
    L = cmp_all.shape[1]
    kcmp = _compress(cmp_all[:, :, 0], cmp_pe[0], cmp_w1[0], cmp_w2[0])
    vcmp = _compress(cmp_all[:, :, 1], cmp_pe[1], cmp_w1[1], cmp_w2[1])
    ns = -(-L // SEL_L)
    slc_pad = jnp.pad(slc_all, ((0, 0), (0, ns * SEL_L - L), (0, 0), (0, 0), (0, 0)))
    slc_pad = slc_pad.reshape(B, ns, SEL_L, 2, NSA_KV_HEADS, HEAD_DIM)
    ksb = jnp.transpose(slc_pad[:, :, :, 0], (0, 3, 1, 2, 4))
    vsb = jnp.transpose(slc_pad[:, :, :, 1], (0, 3, 1, 2, 4))
    n_sel = min(TOP_N, ns)

    def blk(st, n):
        sl = lambda a, m, ax=1: lax.dynamic_slice_in_dim(a, st, m, ax)
        wb = sl(win_all, WINDOW + n)
        return _nsa_block(sl(q, n), sl(pos, n, 0), sl(gates, n), kcmp, vcmp, ksb, vsb,
                          wb[:, :, 0], wb[:, :, 1], sl(win_pos, WINDOW + n, 0), n_sel)
    o = _sweep(blk, T)
    out = o.reshape(B, T, -1) @ w_out
    return out, cmp_new, slc_new, win_state


def setup_inputs(seed: int = 0) -> dict:
    key = jax.random.key(seed)
    k = jax.random.split(key, 32)
    nrm = lambda i, shape, scale: jax.random.normal(k[i], shape, jnp.float32) * scale
    n_pages = PAST_LEN // PAGE_SIZE
    n_used = DEC_BATCH * n_pages
    n_pool = n_used + max(1, n_used // 4)
    win_buf = min(WINDOW, PAST_LEN)
    page_table = jax.random.permutation(k[0], n_pool)[:n_used].reshape(DEC_BATCH, n_pages).astype(jnp.int32)
    return {
        'x_prompt': nrm(1, (BATCH, SEQ, D_MODEL), 1.0),
        'x_sample': nrm(2, (DEC_BATCH, DEC_SEQ, D_MODEL), 1.0),
        'cache_fox_kv': nrm(3, (N_EVEN, n_pool, PAGE_SIZE, 2, FOX_KV_HEADS, HEAD_DIM), 1.0),
        'cache_fox_logf': jax.nn.log_sigmoid(FORGET_BIAS + nrm(4, (N_EVEN, n_pool, PAGE_SIZE, FOX_HEADS), 1.0)),
        'cache_mla': nrm(5, (N_EVEN, n_pool, PAGE_SIZE, MLA_KV_LORA + MLA_ROPE), 1.0),
        'cache_nsa_cmp': nrm(6, (N_ODD, n_pool, PAGE_SIZE, 2, NSA_KV_HEADS, HEAD_DIM), 1.0),
        'cache_nsa_slc': nrm(7, (N_ODD, n_pool, PAGE_SIZE, 2, NSA_KV_HEADS, HEAD_DIM), 1.0),
        'state_nsa_win': nrm(8, (N_ODD, DEC_BATCH, win_buf, 2, NSA_KV_HEADS, HEAD_DIM), 1.0),
        'page_table': page_table,
        'norm_w': 1.0 + nrm(9, (DEPTH, 3, D_MODEL), 0.02),
        'final_norm': 1.0 + nrm(10, (D_MODEL,), 0.02),
        'ffn_w_gate': nrm(11, (DEPTH, 2, D_MODEL, FF_DIM), D_MODEL ** -0.5),
        'ffn_w_up': nrm(12, (DEPTH, 2, D_MODEL, FF_DIM), D_MODEL ** -0.5),
        'ffn_w_down': nrm(13, (DEPTH, 2, FF_DIM, D_MODEL), FF_DIM ** -0.5),
        'ev_w_in': nrm(14, (N_EVEN, D_MODEL, EV_IN), D_MODEL ** -0.5),
        'ev_b_f': FORGET_BIAS + nrm(15, (N_EVEN, FOX_HEADS), 0.1),
        'mla_q_norm': 1.0 + nrm(16, (N_EVEN, MLA_Q_LORA), 0.02),
        'mla_w_q_up': nrm(17, (N_EVEN, MLA_Q_LORA, MLA_HEADS * (MLA_NOPE + MLA_ROPE)), MLA_Q_LORA ** -0.5),
        'mla_kv_norm': 1.0 + nrm(18, (N_EVEN, MLA_KV_LORA), 0.02),
        'mla_w_uk': nrm(19, (N_EVEN, MLA_KV_LORA, MLA_HEADS, MLA_NOPE), MLA_KV_LORA ** -0.5),
        'mla_w_uv': nrm(20, (N_EVEN, MLA_KV_LORA, MLA_HEADS, MLA_V), MLA_KV_LORA ** -0.5),
        'ev_w_out': nrm(21, (N_EVEN, EV_OUT, D_MODEL), EV_OUT ** -0.5),
        'od_w_in': nrm(22, (N_ODD, D_MODEL, OD_IN), D_MODEL ** -0.5),
        'od_b_gate': nrm(23, (N_ODD, 3 * NSA_HEADS), 0.1),
        'nsa_cmp_pe': nrm(24, (N_ODD, 2, CMP_L, HEAD_DIM), 0.02),
        'nsa_cmp_w1': nrm(25, (N_ODD, 2, CMP_L * HEAD_DIM, HEAD_DIM), (CMP_L * HEAD_DIM) ** -0.5),
        'nsa_cmp_w2': nrm(26, (N_ODD, 2, HEAD_DIM, HEAD_DIM), HEAD_DIM ** -0.5),
        'od_w_out': nrm(27, (N_ODD, OD_OUT, D_MODEL), OD_OUT ** -0.5),
    }


def reference(x_prompt, x_sample, cache_fox_kv, cache_fox_logf, cache_mla, cache_nsa_cmp, cache_nsa_slc,
              state_nsa_win, page_table, norm_w, final_norm, ffn_w_gate, ffn_w_up, ffn_w_down,
              ev_w_in, ev_b_f, mla_q_norm, mla_w_q_up, mla_kv_norm, mla_w_uk, mla_w_uv, ev_w_out,
              od_w_in, od_b_gate, nsa_cmp_pe, nsa_cmp_w1, nsa_cmp_w2, od_w_out):
    T_p, T_s = x_prompt.shape[1], x_sample.shape[1]
    past_len = page_table.shape[1] * cache_fox_kv.shape[2]
    pos_p = jnp.arange(T_p)
    pos_s = past_len + jnp.arange(T_s)

    def half_ffn(h, li, j):
        return h + 0.5 * _swiglu(_rms(h, norm_w[li, 2 * j]), ffn_w_gate[li, j], ffn_w_up[li, j], ffn_w_down[li, j])

    fkv_p, fkv_s, flf_p, flf_s, mla_p, mla_s = [], [], [], [], [], []
    cmp_p, cmp_s, slc_p, slc_s, win_p, win_s = [], [], [], [], [], []
    hp, hs = x_prompt, x_sample
    for li in range(DEPTH):
        e = li // 2
        hp, hs = half_ffn(hp, li, 0), half_ffn(hs, li, 0)
        up, us = _rms(hp, norm_w[li, 1]), _rms(hs, norm_w[li, 1])
        if li % 2 == 0:
            w = (ev_w_in[e], ev_b_f[e], mla_q_norm[e], mla_w_q_up[e], mla_kv_norm[e], mla_w_uk[e], mla_w_uv[e], ev_w_out[e])
            past = (_paged(cache_fox_kv, e, page_table), _paged(cache_fox_logf, e, page_table),
                    _paged(cache_mla, e, page_table))
            op, a, b, c = _even_mixer(up, pos_p, None, *w)
            fkv_p.append(a); flf_p.append(b); mla_p.append(c)
            osm, a, b, c = _even_mixer(us, pos_s, past, *w)
            fkv_s.append(a); flf_s.append(b); mla_s.append(c)
        else:
            w = (od_w_in[e], od_b_gate[e], nsa_cmp_pe[e], nsa_cmp_w1[e], nsa_cmp_w2[e], od_w_out[e])
            past = (_paged(cache_nsa_cmp, e, page_table), _paged(cache_nsa_slc, e, page_table), state_nsa_win[e])
            op, a, b, c = _odd_mixer(up, pos_p, None, *w)
            cmp_p.append(a); slc_p.append(b); win_p.append(c)
            osm, a, b, c = _odd_mixer(us, pos_s, past, *w)
            cmp_s.append(a); slc_s.append(b); win_s.append(c)
        hp, hs = hp + op, hs + osm
        hp, hs = half_ffn(hp, li, 1), half_ffn(hs, li, 1)
    y_prompt = _rms(hp, final_norm)
    y_sample = _rms(hs, final_norm)
    return (y_prompt, y_sample,
            jnp.stack(fkv_p), jnp.stack(fkv_s), jnp.stack(flf_p), jnp.stack(flf_s),
            jnp.stack(mla_p), jnp.stack(mla_s), jnp.stack(cmp_p), jnp.stack(cmp_s),
            jnp.stack(slc_p), jnp.stack(slc_s), jnp.stack(win_p), jnp.stack(win_s))
```

```python
import functools
import math

import numpy as np
import jax
import jax.numpy as jnp
from jax import lax
from jax.experimental import pallas as pl
from jax.experimental.pallas import tpu as pltpu

F32 = jnp.float32
BF16 = jnp.bfloat16

D_MODEL = 1024
HEAD_DIM = 64
FOX_HEADS = 8
FOX_KV_HEADS = 2
MLA_HEADS = 8
MLA_Q_LORA = 256
MLA_KV_LORA = 128
MLA_NOPE = 64
MLA_ROPE = 32
MLA_V = 64
MLA_THETA = 10000.0
NSA_HEADS = 16
NSA_KV_HEADS = 2
CMP_L = 32
CMP_D = 16
SEL_L = 64
TOP_N = 16
WINDOW = 512
ROPE_THETA = 500000.0
ROPE_DIM = HEAD_DIM // 4
FF_DIM = 2816
EPS = 1e-6
FORCE_SCORE = 1e9
NEG_INF = -1e30
PAGE = 128

LANES = 128
MLA_LAT = MLA_KV_LORA + MLA_ROPE
MLA_QK = MLA_NOPE + MLA_ROPE
FOX_SCALE = HEAD_DIM ** -0.5
MLA_SCALE = MLA_QK ** -0.5
KV_W = 2 * FOX_KV_HEADS * HEAD_DIM
CHUNK_W = CMP_D * KV_W
VMEM_LIMIT = 56 * 1024 * 1024


def _cp(sem):
    return pltpu.CompilerParams(dimension_semantics=sem, vmem_limit_bytes=VMEM_LIMIT)


def _nt(a, b, precision=None):
    return lax.dot_general(a, b, (((1,), (1,)), ((), ())), precision=precision,
                           preferred_element_type=F32)


def _mm(a, b, precision=None):
    return jnp.dot(a, b, precision=precision, preferred_element_type=F32)


def _rms_val(x, g):
    return (x * lax.rsqrt(jnp.mean(x * x, axis=-1, keepdims=True) + EPS)) * g


def _log_sigmoid(x):
    return -(jnp.maximum(-x, 0.0) + jnp.log1p(jnp.exp(-jnp.abs(x))))


def _gelu_tanh(x):
    return x * (0.5 * (1.0 + jnp.tanh(math.sqrt(2.0 / math.pi) * (x + 0.044715 * (x * x * x)))))


def _iota(shape, dim):
    return lax.broadcasted_iota(jnp.int32, shape, dim)


def _ffn_kernel(x_ref, g_ref, wg_ref, wu_ref, wd_ref, o_ref, xn_ref, acc_ref):
    j = pl.program_id(1)

    @pl.when(j == 0)
    def _():
        xn_ref[...] = _rms_val(x_ref[...], g_ref[...]).astype(BF16)
        acc_ref[...] = jnp.zeros_like(acc_ref)

    xn = xn_ref[...]
    g = _mm(xn, wg_ref[...])
    u = _mm(xn, wu_ref[...])
    a = (g * jax.nn.sigmoid(g)) * u
    acc_ref[...] += _mm(a.astype(BF16), wd_ref[...])

    @pl.when(j == pl.num_programs(1) - 1)
    def _():
        o_ref[...] = x_ref[...] + 0.5 * acc_ref[...]


def _ffn(h, g, wg, wu, wd, tm, tf):
    n = h.shape[0]
    return pl.pallas_call(
        _ffn_kernel,
        grid=(n // tm, FF_DIM // tf),
        in_specs=[
            pl.BlockSpec((tm, D_MODEL), lambda i, j: (i, 0)),
            pl.BlockSpec((1, D_MODEL), lambda i, j: (0, 0)),
            pl.BlockSpec((D_MODEL, tf), lambda i, j: (0, j)),
            pl.BlockSpec((D_MODEL, tf), lambda i, j: (0, j)),
            pl.BlockSpec((tf, D_MODEL), lambda i, j: (j, 0)),
        ],
        out_specs=pl.BlockSpec((tm, D_MODEL), lambda i, j: (i, 0)),
        out_shape=jax.ShapeDtypeStruct((n, D_MODEL), F32),
        scratch_shapes=[pltpu.VMEM((tm, D_MODEL), BF16), pltpu.VMEM((tm, D_MODEL), F32)],
        compiler_params=_cp(("parallel", "arbitrary")),
        name="ffn",
    )(h, g, wg, wu, wd)


def _out_proj_kernel(h_ref, op_ref, os_ref, w_ref, o_ref, *, npt):
    i = pl.program_id(0)

    @pl.when(i < npt)
    def _():
        o_ref[...] = h_ref[...] + _mm(op_ref[...], w_ref[...])

    @pl.when(i >= npt)
    def _():
        o_ref[...] = h_ref[...] + _mm(os_ref[...].astype(BF16), w_ref[...])


def _out_proj(h, o_p, o_s, w, tm):
    n = h.shape[0]
    npt = o_p.shape[0] // tm
    return pl.pallas_call(
        functools.partial(_out_proj_kernel, npt=npt),
        grid=(n // tm,),
        in_specs=[
            pl.BlockSpec((tm, D_MODEL), lambda i: (i, 0)),
            pl.BlockSpec((tm, D_MODEL), lambda i: (jnp.minimum(i, npt - 1), 0)),
            pl.BlockSpec((tm, D_MODEL), lambda i: (jnp.maximum(i - npt, 0), 0)),
            pl.BlockSpec((D_MODEL, D_MODEL), lambda i: (0, 0)),
        ],
        out_specs=pl.BlockSpec((tm, D_MODEL), lambda i: (i, 0)),
        out_shape=jax.ShapeDtypeStruct((n, D_MODEL), F32),
        compiler_params=_cp(("parallel",)),
        name="out_proj",
    )(h, o_p, o_s, w)


def _final_norm_kernel(x_ref, g_ref, o_ref):
    o_ref[...] = _rms_val(x_ref[...], g_ref[...])


def _final_norm(h, g, tm):
    n = h.shape[0]
    return pl.pallas_call(
        _final_norm_kernel,
        grid=(n // tm,),
        in_specs=[pl.BlockSpec((tm, D_MODEL), lambda i: (i, 0)),
                  pl.BlockSpec((1, D_MODEL), lambda i: (0, 0))],
        out_specs=pl.BlockSpec((tm, D_MODEL), lambda i: (i, 0)),
        out_shape=jax.ShapeDtypeStruct((n, D_MODEL), F32),
        compiler_params=_cp(("parallel",)),
        name="final_norm",
    )(h, g)


def _rope_tables(t_prompt, past_len, t_dec, tm, theta, rot_dim, period):
    half = rot_dim // 2
    pos = np.concatenate([np.arange(t_prompt), past_len + (np.arange(tm) % t_dec)]).astype(np.float64)
    inv = np.float64(theta) ** (-np.arange(half, dtype=np.float64) / half)
    ang = pos[:, None] * inv[None, :]
    cos, sin = np.cos(ang), np.sin(ang)
    lane = np.arange(LANES) % period
    c = np.ones((pos.shape[0], LANES))
    s1 = np.zeros((pos.shape[0], LANES))
    s2 = np.zeros((pos.shape[0], LANES))
    lo = lane < half
    hi = (lane >= half) & (lane < rot_dim)
    c[:, lo] = cos[:, lane[lo]]
    c[:, hi] = cos[:, lane[hi] - half]
    s1[:, hi] = sin[:, lane[hi] - half]
    s2[:, lo] = -sin[:, lane[lo]]
    return jnp.asarray(np.stack([c, s1, s2]), dtype=F32)


def _rope_chunk(x, rope_ref, half):
    return (x * rope_ref[0] + pltpu.roll(x, half, 1) * rope_ref[1]
            + pltpu.roll(x, LANES - half, 1) * rope_ref[2])


def _rope_spec(tm, t_prompt, npt):
    tiles = t_prompt // tm
    return pl.BlockSpec((3, tm, LANES), lambda i: (0, jnp.where(i < npt, i % tiles, tiles), 0))


EV_COLS = 1280
EV_FLG_LANE = 32


def _even_proj_kernel(x_ref, g_ref, win_ref, bf_ref, qn_ref, wqu_ref, wuk_ref, kvn_ref, rope_ref,
                      qf_ref, kv32_ref, kv16_ref, logf_ref, qm_ref, lat32_ref, lat16_ref):
    xn = _rms_val(x_ref[...], g_ref[...]).astype(BF16)
    z = _mm(xn, win_ref[...])
    for h in range(FOX_HEADS):
        qf_ref[h] = z[:, HEAD_DIM * h:HEAD_DIM * (h + 1)] * FOX_SCALE
    kv = z[:, 512:768]
    kv32_ref[...] = kv
    for j in range(4):
        kv16_ref[j] = kv[:, HEAD_DIM * j:HEAD_DIM * (j + 1)].astype(BF16)
    cq = _rms_val(z[:, 768:1024], qn_ref[...]).astype(BF16)
    qmm = _mm(cq, wqu_ref[...])
    qlat = _mm(qmm[:, :512].astype(BF16), wuk_ref[...]) * MLA_SCALE
    for h in range(MLA_HEADS):
        qm_ref[h, :, 0:MLA_KV_LORA] = qlat[:, MLA_KV_LORA * h:MLA_KV_LORA * (h + 1)]
    for c in range(2):
        pe = _rope_chunk(qmm[:, 512 + LANES * c:512 + LANES * (c + 1)], rope_ref, MLA_ROPE // 2) * MLA_SCALE
        for hh in range(4):
            qm_ref[4 * c + hh, :, MLA_KV_LORA:MLA_LAT] = pe[:, MLA_ROPE * hh:MLA_ROPE * (hh + 1)]
    ckv = _rms_val(z[:, 1024:1152], kvn_ref[...])
    last = z[:, 1152:1280]
    kpe = _rope_chunk(last, rope_ref, MLA_ROPE // 2)[:, 0:MLA_ROPE]
    lat32_ref[:, 0:MLA_KV_LORA] = ckv
    lat32_ref[:, MLA_KV_LORA:MLA_LAT] = kpe
    lat16_ref[:, 0:MLA_KV_LORA] = ckv.astype(BF16)
    lat16_ref[:, MLA_KV_LORA:MLA_LAT] = kpe.astype(BF16)
    logf_ref[...] = _log_sigmoid(last + bf_ref[...])[:, EV_FLG_LANE:EV_FLG_LANE + FOX_HEADS]


def _even_proj(h, g, w_in, b_f, q_norm, w_q_up, wuk_bd, kv_norm, rope, tm, t_prompt, n_p):
    n = h.shape[0]
    npt = n_p // tm
    full = lambda shape: pl.BlockSpec(shape, lambda i: (0,) * len(shape))
    return pl.pallas_call(
        _even_proj_kernel,
        grid=(n // tm,),
        in_specs=[
            pl.BlockSpec((tm, D_MODEL), lambda i: (i, 0)),
            full((1, D_MODEL)), full((D_MODEL, EV_COLS)), full((1, LANES)), full((1, MLA_Q_LORA)),
            full((MLA_Q_LORA, 768)), full((512, 1024)), full((1, MLA_KV_LORA)),
            _rope_spec(tm, t_prompt, npt),
        ],
        out_specs=[
            pl.BlockSpec((FOX_HEADS, tm, HEAD_DIM), lambda i: (0, i, 0)),
            pl.BlockSpec((tm, KV_W), lambda i: (i, 0)),
            pl.BlockSpec((4, tm, HEAD_DIM), lambda i: (0, i, 0)),
            pl.BlockSpec((tm, FOX_HEADS), lambda i: (i, 0)),
            pl.BlockSpec((MLA_HEADS, tm, MLA_LAT), lambda i: (0, i, 0)),
            pl.BlockSpec((tm, MLA_LAT), lambda i: (i, 0)),
            pl.BlockSpec((tm, MLA_LAT), lambda i: (i, 0)),
        ],
        out_shape=[
            jax.ShapeDtypeStruct((FOX_HEADS, n, HEAD_DIM), F32),
            jax.ShapeDtypeStruct((n, KV_W), F32),
            jax.ShapeDtypeStruct((4, n, HEAD_DIM), BF16),
            jax.ShapeDtypeStruct((n, FOX_HEADS), F32),
            jax.ShapeDtypeStruct((MLA_HEADS, n, MLA_LAT), F32),
            jax.ShapeDtypeStruct((n, MLA_LAT), F32),
            jax.ShapeDtypeStruct((n, MLA_LAT), BF16),
        ],
        compiler_params=_cp(("parallel",)),
        name="even_proj",
    )(h, g, w_in, b_f, q_norm, w_q_up, wuk_bd, kv_norm, rope)


CS_CHUNK = 128


def _csum_kernel(lf_ref, cs_ref, cst_ref, *, t, tk):
    r = _iota((CS_CHUNK, CS_CHUNK), 0)
    c = _iota((CS_CHUNK, CS_CHUNK), 1)
    tri = (c <= r).astype(F32)
    eye = (_iota((FOX_HEADS, FOX_HEADS), 0) == _iota((FOX_HEADS, FOX_HEADS), 1)).astype(F32)
    carry = jnp.zeros((1, FOX_HEADS), F32)
    per = tk // CS_CHUNK
    for k in range(t // CS_CHUNK):
        lf = lf_ref[k * CS_CHUNK:(k + 1) * CS_CHUNK, :]
        cs = _mm(tri, lf, lax.Precision.HIGHEST) + carry
        carry = cs[CS_CHUNK - 1:CS_CHUNK, :]
        cs_ref[k * CS_CHUNK:(k + 1) * CS_CHUNK, :] = cs
        cst_ref[k // per, :, (k % per) * CS_CHUNK:(k % per + 1) * CS_CHUNK] = _nt(eye, cs, lax.Precision.HIGHEST)


def _csum(logf, b, t, tk):
    return pl.pallas_call(
        functools.partial(_csum_kernel, t=t, tk=tk),
        grid=(b,),
        in_specs=[pl.BlockSpec((t, FOX_HEADS), lambda i: (i, 0))],
        out_specs=[pl.BlockSpec((t, FOX_HEADS), lambda i: (i, 0)),
                   pl.BlockSpec((None, t // tk, FOX_HEADS, tk), lambda i: (i, 0, 0, 0))],
        out_shape=[jax.ShapeDtypeStruct((b * t, FOX_HEADS), F32),
                   jax.ShapeDtypeStruct((b, t // tk, FOX_HEADS, tk), F32)],
        compiler_params=_cp(("parallel",)),
        name="fox_csum",
    )(logf)


def _softmax_step(s, m_ref, l_ref, acc_ref, v, idx):
    m_prev = m_ref[idx]
    m_new = jnp.maximum(m_prev, jnp.max(s, axis=-1, keepdims=True))
    alpha = jnp.exp(m_prev - m_new)
    p = jnp.exp(s - m_new)
    l_ref[idx] = alpha * l_ref[idx] + jnp.sum(p, axis=-1, keepdims=True)
    acc_ref[idx] = alpha * acc_ref[idx] + _mm(p.astype(BF16), v)
    m_ref[idx] = m_new


def _even_attn_kernel(qf_ref, qm_ref, cs_ref, kv_ref, lat_ref, cst_ref, wuv_ref, o_ref,
                      mf_ref, lf_ref, af_ref, mm_ref, lm_ref, am_ref, *, tq, tk):
    qi = pl.program_id(1)
    qs = qi * tq
    rq = FOX_HEADS // FOX_KV_HEADS
    mf_ref[...] = jnp.full_like(mf_ref, NEG_INF)
    lf_ref[...] = jnp.zeros_like(lf_ref)
    af_ref[...] = jnp.zeros_like(af_ref)
    mm_ref[...] = jnp.full_like(mm_ref, NEG_INF)
    lm_ref[...] = jnp.zeros_like(lm_ref)
    am_ref[...] = jnp.zeros_like(am_ref)
    cs = cs_ref[...]
    qg = [qf_ref[rq * g:rq * (g + 1)].reshape(rq * tq, HEAD_DIM).astype(BF16) for g in range(FOX_KV_HEADS)]
    qm = qm_ref[...].reshape(MLA_HEADS * tq, MLA_LAT).astype(BF16)
    qpos = qs + _iota((tq, tk), 0)

    def step(j, masked):
        ks = pl.multiple_of(j * tk, tk)
        if masked:
            keep = (ks + _iota((tq, tk), 1)) <= qpos
        ck = cst_ref[j]
        for g in range(FOX_KV_HEADS):
            k = kv_ref[g, pl.ds(ks, tk), :]
            v = kv_ref[FOX_KV_HEADS + g, pl.ds(ks, tk), :]
            s = _nt(qg[g], k)
            parts = []
            for r in range(rq):
                h = rq * g + r
                sr = s[r * tq:(r + 1) * tq] + (cs[:, h:h + 1] - ck[h:h + 1, :])
                parts.append(jnp.where(keep, sr, NEG_INF) if masked else sr)
            _softmax_step(jnp.concatenate(parts, axis=0), mf_ref, lf_ref, af_ref, v, g)
        lat = lat_ref[pl.ds(ks, tk), :]
        s = _nt(qm, lat)
        if masked:
            s = jnp.where(keep[None], s.reshape(MLA_HEADS, tq, tk), NEG_INF).reshape(MLA_HEADS * tq, tk)
        _softmax_step(s, mm_ref, lm_ref, am_ref, lat[:, :MLA_KV_LORA], 0)

    n_full = qs // tk
    n_all = (qs + tq - 1) // tk + 1

    def full_body(j, c):
        step(j, False)
        return c

    def diag_body(j, c):
        step(j, True)
        return c

    lax.fori_loop(0, n_full, full_body, 0)
    lax.fori_loop(n_full, n_all, diag_body, 0)

    for g in range(FOX_KV_HEADS):
        o = af_ref[g] / lf_ref[g]
        for r in range(rq):
            h = rq * g + r
            o_ref[:, HEAD_DIM * h:HEAD_DIM * (h + 1)] = o[r * tq:(r + 1) * tq].astype(BF16)
    ol = (am_ref[0] / lm_ref[0]).astype(BF16)
    base = FOX_HEADS * HEAD_DIM
    for h in range(MLA_HEADS):
        om = _mm(ol[h * tq:(h + 1) * tq], wuv_ref[h])
        o_ref[:, base + MLA_V * h:base + MLA_V * (h + 1)] = om.astype(BF16)


def _even_attn(qf, qm, csum, kv16, lat16, cst, wuv, b, t, tq, tk):
    nq = t // tq
    rq = FOX_HEADS // FOX_KV_HEADS
    return pl.pallas_call(
        functools.partial(_even_attn_kernel, tq=tq, tk=tk),
        grid=(b, nq),
        in_specs=[
            pl.BlockSpec((FOX_HEADS, tq, HEAD_DIM), lambda i, j: (0, i * nq + j, 0)),
            pl.BlockSpec((MLA_HEADS, tq, MLA_LAT), lambda i, j: (0, i * nq + j, 0)),
            pl.BlockSpec((tq, FOX_HEADS), lambda i, j: (i * nq + j, 0)),
            pl.BlockSpec((4, t, HEAD_DIM), lambda i, j: (0, i, 0)),
            pl.BlockSpec((t, MLA_LAT), lambda i, j: (i, 0)),
            pl.BlockSpec((None, t // tk, FOX_HEADS, tk), lambda i, j: (i, 0, 0, 0)),
            pl.BlockSpec((MLA_HEADS, MLA_KV_LORA, MLA_V), lambda i, j: (0, 0, 0)),
        ],
        out_specs=pl.BlockSpec((tq, D_MODEL), lambda i, j: (i * nq + j, 0)),
        out_shape=jax.ShapeDtypeStruct((b * t, D_MODEL), BF16),
        scratch_shapes=[
            pltpu.VMEM((FOX_KV_HEADS, rq * tq, 1), F32), pltpu.VMEM((FOX_KV_HEADS, rq * tq, 1), F32),
            pltpu.VMEM((FOX_KV_HEADS, rq * tq, HEAD_DIM), F32),
            pltpu.VMEM((1, MLA_HEADS * tq, 1), F32), pltpu.VMEM((1, MLA_HEADS * tq, 1), F32),
            pltpu.VMEM((1, MLA_HEADS * tq, MLA_KV_LORA), F32),
        ],
        compiler_params=_cp(("parallel", "arbitrary")),
        name="even_attn_prompt",
    )(qf, qm, csum, kv16, lat16, cst, wuv)


def _lane_cumsum(x):
    lane = _iota(x.shape, 1)
    d = 1
    while d < LANES:
        x = x + jnp.where(lane >= d, pltpu.roll(x, d, 1), 0.0)
        d *= 2
    return x


def _even_dec_kernel(pt_ref, qf_ref, qm_ref, kvn_ref, latn_ref, lfn_ref, wuv_ref, *rest, pg, td):
    kv_refs = rest[:pg]
    lf_refs = rest[pg:2 * pg]
    lat_refs = rest[2 * pg:3 * pg]
    o_ref = rest[3 * pg]
    mf_ref, lf_ref, af_ref, mm_ref, lm_ref, am_ref, car_ref = rest[3 * pg + 1:]
    c = pl.program_id(1)
    rq = FOX_HEADS // FOX_KV_HEADS
    rows = rq * td

    @pl.when(c == 0)
    def _():
        mf_ref[...] = jnp.full_like(mf_ref, NEG_INF)
        lf_ref[...] = jnp.zeros_like(lf_ref)
        af_ref[...] = jnp.zeros_like(af_ref)
        mm_ref[...] = jnp.full_like(mm_ref, NEG_INF)
        lm_ref[...] = jnp.zeros_like(lm_ref)
        am_ref[...] = jnp.zeros_like(am_ref)
        car_ref[...] = jnp.zeros_like(car_ref)

    carry = car_ref[...]
    cums = []
    for i in range(pg):
        ci = _lane_cumsum(lf_refs[i][...]) + carry
        carry = jnp.broadcast_to(ci[:, LANES - 1:LANES], ci.shape)
        cums.append(ci)
    car_ref[...] = carry
    ck = jnp.concatenate(cums, axis=1)

    def fox_update(g, s, vs):
        m_prev = mf_ref[g]
        m_new = jnp.maximum(m_prev, jnp.max(s, axis=-1, keepdims=True))
        alpha = jnp.exp(m_prev - m_new)
        p = jnp.exp(s - m_new)
        lf_ref[g] = alpha * lf_ref[g] + jnp.sum(p, axis=-1, keepdims=True)
        w = vs[0].shape[0]
        pv = _mm(p[:, 0:w].astype(BF16), vs[0])
        for i in range(1, len(vs)):
            pv = pv + _mm(p[:, i * w:(i + 1) * w].astype(BF16), vs[i])
        af_ref[g] = alpha * af_ref[g] + pv
        mf_ref[g] = m_new

    def mla_update(s, vs):
        m_prev = mm_ref[...]
        m_new = jnp.maximum(m_prev, jnp.max(s, axis=-1, keepdims=True))
        alpha = jnp.exp(m_prev - m_new)
        p = jnp.exp(s - m_new)
        lm_ref[...] = alpha * lm_ref[...] + jnp.sum(p, axis=-1, keepdims=True)
        w = vs[0].shape[0]
        pv = _mm(p[:, 0:w].astype(BF16), vs[0])
        for i in range(1, len(vs)):
            pv = pv + _mm(p[:, i * w:(i + 1) * w].astype(BF16), vs[i])
        am_ref[...] = alpha * am_ref[...] + pv
        mm_ref[...] = m_new

    kvs = [kv_refs[i][...].astype(BF16) for i in range(pg)]
    qgs = [qf_ref[rq * g:rq * (g + 1)].reshape(rows, HEAD_DIM).astype(BF16) for g in range(FOX_KV_HEADS)]
    for g in range(FOX_KV_HEADS):
        s = jnp.concatenate([_nt(qgs[g], kvs[i][:, HEAD_DIM * g:HEAD_DIM * (g + 1)]) for i in range(pg)], axis=1)
        s = jnp.concatenate([s[r * td:(r + 1) * td] - ck[rq * g + r:rq * g + r + 1, :] for r in range(rq)], axis=0)
        vo = (FOX_KV_HEADS + g) * HEAD_DIM
        fox_update(g, s, [kvs[i][:, vo:vo + HEAD_DIM] for i in range(pg)])
    lats = [lat_refs[i][...].astype(BF16) for i in range(pg)]
    qm = qm_ref[...].reshape(MLA_HEADS * td, MLA_LAT).astype(BF16)
    s = jnp.concatenate([_nt(qm, lats[i]) for i in range(pg)], axis=1)
    mla_update(s, [lats[i][:, :MLA_KV_LORA] for i in range(pg)])

    @pl.when(c == pl.num_programs(1) - 1)
    def _():
        tri = (_iota((td, td), 1) <= _iota((td, td), 0)).astype(F32)
        eye = (_iota((FOX_HEADS, FOX_HEADS), 0) == _iota((FOX_HEADS, FOX_HEADS), 1)).astype(F32)
        csn = _mm(tri, lfn_ref[...], lax.Precision.HIGHEST)
        ckn = _nt(eye, csn, lax.Precision.HIGHEST) + car_ref[:, 0:td]
        kvn = kvn_ref[...].astype(BF16)
        t_row = _iota((rows, td), 0) % td
        keep = _iota((rows, td), 1) <= t_row
        for g in range(FOX_KV_HEADS):
            s = _nt(qgs[g], kvn[:, HEAD_DIM * g:HEAD_DIM * (g + 1)])
            s = jnp.concatenate([s[r * td:(r + 1) * td] - ckn[rq * g + r:rq * g + r + 1, :] for r in range(rq)], axis=0)
            s = jnp.where(keep, s, NEG_INF)
            vo = (FOX_KV_HEADS + g) * HEAD_DIM
            fox_update(g, s, [kvn[:, vo:vo + HEAD_DIM]])
        latn = latn_ref[...].astype(BF16)
        t_row_m = _iota((MLA_HEADS * td, td), 0) % td
        keep_m = _iota((MLA_HEADS * td, td), 1) <= t_row_m
        s = jnp.where(keep_m, _nt(qm, latn), NEG_INF)
        mla_update(s, [latn[:, :MLA_KV_LORA]])
        for g in range(FOX_KV_HEADS):
            o = af_ref[g] / lf_ref[g]
            for r in range(rq):
                h = rq * g + r
                o_ref[:, HEAD_DIM * h:HEAD_DIM * (h + 1)] = o[r * td:(r + 1) * td]
        ol = (am_ref[...] / lm_ref[...]).astype(BF16)
        base = FOX_HEADS * HEAD_DIM
        for h in range(MLA_HEADS):
            o_ref[:, base + MLA_V * h:base + MLA_V * (h + 1)] = _mm(ol[h * td:(h + 1) * td], wuv_ref[h])


def _even_dec(page_table, qf, qm, kv32, lat32, logf, wuv, cache_kv, cache_lft, cache_lat, e, n_p, bd, td, pg):
    n_pages = page_table.shape[1]
    off = n_p // td
    rq = FOX_HEADS // FOX_KV_HEADS

    def page_spec(shape, i):
        return pl.BlockSpec((None, None) + shape, lambda b, c, pt: (e, pt[b, c * pg + i], 0, 0))

    in_specs = [
        pl.BlockSpec((FOX_HEADS, td, HEAD_DIM), lambda b, c, pt: (0, off + b, 0)),
        pl.BlockSpec((MLA_HEADS, td, MLA_LAT), lambda b, c, pt: (0, off + b, 0)),
        pl.BlockSpec((td, KV_W), lambda b, c, pt: (off + b, 0)),
        pl.BlockSpec((td, MLA_LAT), lambda b, c, pt: (off + b, 0)),
        pl.BlockSpec((td, FOX_HEADS), lambda b, c, pt: (off + b, 0)),
        pl.BlockSpec((MLA_HEADS, MLA_KV_LORA, MLA_V), lambda b, c, pt: (0, 0, 0)),
    ]
    in_specs += [page_spec((PAGE, KV_W), i) for i in range(pg)]
    in_specs += [page_spec((FOX_HEADS, PAGE), i) for i in range(pg)]
    in_specs += [page_spec((PAGE, MLA_LAT), i) for i in range(pg)]
    grid_spec = pltpu.PrefetchScalarGridSpec(
        num_scalar_prefetch=1,
        grid=(bd, n_pages // pg),
        in_specs=in_specs,
        out_specs=pl.BlockSpec((td, D_MODEL), lambda b, c, pt: (b, 0)),
        scratch_shapes=[
            pltpu.VMEM((FOX_KV_HEADS, rq * td, 1), F32), pltpu.VMEM((FOX_KV_HEADS, rq * td, 1), F32),
            pltpu.VMEM((FOX_KV_HEADS, rq * td, HEAD_DIM), F32),
            pltpu.VMEM((MLA_HEADS * td, 1), F32), pltpu.VMEM((MLA_HEADS * td, 1), F32),
            pltpu.VMEM((MLA_HEADS * td, MLA_KV_LORA), F32),
            pltpu.VMEM((FOX_HEADS, LANES), F32),
        ],
    )
    return pl.pallas_call(
        functools.partial(_even_dec_kernel, pg=pg, td=td),
        grid_spec=grid_spec,
        out_shape=jax.ShapeDtypeStruct((bd * td, D_MODEL), F32),
        compiler_params=_cp(("parallel", "arbitrary")),
        name="even_attn_decode",
    )(page_table, qf, qm, kv32, lat32, logf, wuv,
      *([cache_kv] * pg), *([cache_lft] * pg), *([cache_lat] * pg))


OD_COLS = 1920
N_GATES = 3 * NSA_HEADS


def _odd_proj_kernel(x_ref, g_ref, win_ref, bg_ref, rope_ref,
                     q_ref, cmp_ref, slc_ref, wn_ref, slc16_ref, wn16_ref, gate_ref):
    xn = _rms_val(x_ref[...], g_ref[...]).astype(BF16)
    z = _mm(xn, win_ref[...])
    half = ROPE_DIM // 2
    for c in range(NSA_HEADS // 2):
        qc = _rope_chunk(z[:, LANES * c:LANES * (c + 1)], rope_ref, half) * FOX_SCALE
        q_ref[2 * c] = qc[:, :HEAD_DIM]
        q_ref[2 * c + 1] = qc[:, HEAD_DIM:]
    ko, vo = NSA_HEADS * HEAD_DIM, NSA_HEADS * HEAD_DIM + 3 * LANES
    for n, (r32, r16) in enumerate(((cmp_ref, None), (slc_ref, slc16_ref), (wn_ref, wn16_ref))):
        k = _rope_chunk(z[:, ko + LANES * n:ko + LANES * (n + 1)], rope_ref, half)
        v = z[:, vo + LANES * n:vo + LANES * (n + 1)]
        r32[:, 0:LANES] = k
        r32[:, LANES:2 * LANES] = v
        if r16 is not None:
            for g in range(NSA_KV_HEADS):
                r16[g] = k[:, HEAD_DIM * g:HEAD_DIM * (g + 1)].astype(BF16)
                r16[NSA_KV_HEADS + g] = v[:, HEAD_DIM * g:HEAD_DIM * (g + 1)].astype(BF16)
    gate_ref[...] = jax.nn.sigmoid(z[:, vo + 3 * LANES:vo + 4 * LANES] + bg_ref[...])[:, 0:N_GATES]


def _odd_proj(h, g, w_in, b_gate, rope, tm, t_prompt, n_p):
    n = h.shape[0]
    npt = n_p // tm
    full = lambda shape: pl.BlockSpec(shape, lambda i: (0,) * len(shape))
    row = lambda w: pl.BlockSpec((tm, w), lambda i: (i, 0))
    hm = lambda k: pl.BlockSpec((k, tm, HEAD_DIM), lambda i: (0, i, 0))
    return pl.pallas_call(
        _odd_proj_kernel,
        grid=(n // tm,),
        in_specs=[row(D_MODEL), full((1, D_MODEL)), full((D_MODEL, OD_COLS)), full((1, LANES)),
                  _rope_spec(tm, t_prompt, npt)],
        out_specs=[hm(NSA_HEADS), row(KV_W), row(KV_W), row(KV_W), hm(4), hm(4), row(N_GATES)],
        out_shape=[
            jax.ShapeDtypeStruct((NSA_HEADS, n, HEAD_DIM), F32),
            jax.ShapeDtypeStruct((n, KV_W), F32), jax.ShapeDtypeStruct((n, KV_W), F32),
            jax.ShapeDtypeStruct((n, KV_W), F32),
            jax.ShapeDtypeStruct((4, n, HEAD_DIM), BF16), jax.ShapeDtypeStruct((4, n, HEAD_DIM), BF16),
            jax.ShapeDtypeStruct((n, N_GATES), F32),
        ],
        compiler_params=_cp(("parallel",)),
        name="odd_proj",
    )(h, g, w_in, b_gate, rope)


def _compress_rows(ch, pe_ref, wlo_ref, whi_ref, w2_ref):
    a = _mm((ch + pe_ref[0:1, :]).astype(BF16), wlo_ref[...])
    b = _mm((ch + pe_ref[1:2, :]).astype(BF16), whi_ref[...])
    hid = a + pltpu.roll(b, ch.shape[0] - 1, 0)
    return _mm(_gelu_tanh(hid).astype(BF16), w2_ref[...])


def _compress_prompt_kernel(ch_ref, pe_ref, wlo_ref, whi_ref, w2_ref, o_ref):
    o_ref[...] = _compress_rows(ch_ref[...], pe_ref, wlo_ref, whi_ref, w2_ref).astype(BF16)


def _compress_prompt(chunks, pe, wlo, whi, w2, b, rows):
    full = lambda shape: pl.BlockSpec(shape, lambda i: (0,) * len(shape))
    return pl.pallas_call(
        _compress_prompt_kernel,
        grid=(b,),
        in_specs=[pl.BlockSpec((rows, CHUNK_W), lambda i: (i, 0)),
                  full((2, CHUNK_W)), full((CHUNK_W, KV_W)), full((CHUNK_W, KV_W)), full((KV_W, KV_W))],
        out_specs=pl.BlockSpec((rows, KV_W), lambda i: (i, 0)),
        out_shape=jax.ShapeDtypeStruct((b * rows, KV_W), BF16),
        compiler_params=_cp(("parallel",)),
        name="nsa_compress_prompt",
    )(chunks, pe, wlo, whi, w2)


def _cmp_branch(q, kc, vc, qpos_col, heads, t):
    ncp = kc.shape[0]
    s = _nt(q, kc).reshape(heads, t, ncp)
    mc = (_iota((t, ncp), 1) * CMP_D + (CMP_L - 1)) <= qpos_col
    s = jnp.where(mc[None], s, NEG_INF)
    e = jnp.exp(s - jnp.max(s, axis=-1, keepdims=True))
    p = jnp.where(mc[None], e / jnp.sum(e, axis=-1, keepdims=True), 0.0)
    o = _mm(p.reshape(heads * t, ncp).astype(BF16), vc)
    return o, jnp.sum(p, axis=0)


def _select_blocks(imp_t, qpos_row, sc_ref, n_real):
    ns, c = imp_t.shape
    blk = _iota((ns, c), 0)
    cur = qpos_row // SEL_L
    forced = (blk == 0) | (blk == cur) | (blk == cur - 1)
    valid = (blk * SEL_L <= qpos_row) & (blk < n_real)
    score = jnp.where(valid, jnp.where(forced, FORCE_SCORE, imp_t), -1.0)
    score = jnp.where(blk < n_real, score, -2.0)
    sc_ref[...] = score

    def body(i, rank):
        row = sc_ref[pl.ds(i, 1), :]
        ahead = (row > score) | ((row == score) & (i < blk))
        return rank + ahead.astype(F32)

    rank = lax.fori_loop(0, n_real, body, jnp.zeros((ns, c), F32))
    return (rank < float(min(TOP_N, n_real))).astype(F32)


def _imp_matrix(ns_pad, ncp):
    ratio, span = SEL_L // CMP_D, CMP_L // CMP_D
    m = np.zeros((ns_pad, ncp), np.float32)
    for j in range(ns_pad):
        for a in range(ratio):
            for b in range(span):
                cc = j * ratio + a + b
                if cc < ncp:
                    m[j, cc] += 1.0
    return jnp.asarray(m)


def _expand_matrix(ns_pad, n_keys):
    m = (np.arange(n_keys)[None, :] // SEL_L == np.arange(ns_pad)[:, None]).astype(np.float32)
    return jnp.asarray(m, dtype=BF16)


def _nsa_prompt_kernel(q_ref, gate_ref, kvc_ref, slc_ref, wn_ref, mimp_ref, eye_ref, exp_ref, o_ref,
                       m_ref, l_ref, acc_ref, mask_ref, sc_ref, *, tq, tk, t):
    qi = pl.program_id(1)
    qs = qi * tq
    rq = NSA_HEADS // NSA_KV_HEADS
    ns = t // SEL_L
    qpos_col = qs + _iota((tq, 1), 0)
    qpos_row = qs + _iota((1, tq), 1)
    qpos = qs + _iota((tq, tk), 0)
    kvc = kvc_ref[...]
    gates = gate_ref[...]

    def reset():
        m_ref[...] = jnp.full_like(m_ref, NEG_INF)
        l_ref[...] = jnp.zeros_like(l_ref)
        acc_ref[...] = jnp.zeros_like(acc_ref)

    for g in range(NSA_KV_HEADS):
        q = q_ref[rq * g:rq * (g + 1)].reshape(rq * tq, HEAD_DIM).astype(BF16)
        kc = kvc[:, HEAD_DIM * g:HEAD_DIM * (g + 1)]
        vc = kvc[:, HEAD_DIM * (NSA_KV_HEADS + g):HEAD_DIM * (NSA_KV_HEADS + g + 1)]
        o_cmp, pgrp = _cmp_branch(q, kc, vc, qpos_col, rq, tq)
        imp_t = _nt(mimp_ref[...], pgrp, lax.Precision.HIGHEST)
        sel_t = _select_blocks(imp_t, qpos_row, sc_ref, ns)
        sel = _nt(eye_ref[...], sel_t.astype(BF16))
        mfull = _mm(sel.astype(BF16), exp_ref[...])
        for jj in range(t // tk):
            mask_ref[jj] = mfull[:, jj * tk:(jj + 1) * tk]

        reset()

        def slc_body(j, c):
            ks = pl.multiple_of(j * tk, tk)
            k = slc_ref[g, pl.ds(ks, tk), :]
            v = slc_ref[NSA_KV_HEADS + g, pl.ds(ks, tk), :]
            keep = (mask_ref[j] > 0.5) & ((ks + _iota((tq, tk), 1)) <= qpos)
            s = jnp.where(keep[None], _nt(q, k).reshape(rq, tq, tk), NEG_INF)
            _softmax_step(s.reshape(rq * tq, tk), m_ref, l_ref, acc_ref, v, 0)
            return c

        n_all = (qs + tq - 1) // tk + 1
        lax.fori_loop(0, n_all, slc_body, 0)
        o_slc = acc_ref[0] / l_ref[0]

        reset()

        def win_body(j, c):
            ks = pl.multiple_of(j * tk, tk)
            k = wn_ref[g, pl.ds(ks, tk), :]
            v = wn_ref[NSA_KV_HEADS + g, pl.ds(ks, tk), :]
            d = qpos - (ks + _iota((tq, tk), 1))
            keep = (d >= 0) & (d < WINDOW)
            s = jnp.where(keep[None], _nt(q, k).reshape(rq, tq, tk), NEG_INF)
            _softmax_step(s.reshape(rq * tq, tk), m_ref, l_ref, acc_ref, v, 0)
            return c

        j0 = jnp.maximum(qs - (WINDOW - 1), 0) // tk
        lax.fori_loop(j0, n_all, win_body, 0)
        o_win = acc_ref[0] / l_ref[0]

        for r in range(rq):
            h = rq * g + r
            rows = slice(r * tq, (r + 1) * tq)
            o = (gates[:, h:h + 1] * o_cmp[rows] + gates[:, NSA_HEADS + h:NSA_HEADS + h + 1] * o_slc[rows]
                 + gates[:, 2 * NSA_HEADS + h:2 * NSA_HEADS + h + 1] * o_win[rows])
            o_ref[:, HEAD_DIM * h:HEAD_DIM * (h + 1)] = o.astype(BF16)


def _nsa_prompt(q, gates, kvc, slc16, wn16, mimp, eye, expand, b, t, tq, tk):
    nq = t // tq
    rq = NSA_HEADS // NSA_KV_HEADS
    ncp = t // CMP_D
    ns = t // SEL_L
    return pl.pallas_call(
        functools.partial(_nsa_prompt_kernel, tq=tq, tk=tk, t=t),
        grid=(b, nq),
        in_specs=[
            pl.BlockSpec((NSA_HEADS, tq, HEAD_DIM), lambda i, j: (0, i * nq + j, 0)),
            pl.BlockSpec((tq, N_GATES), lambda i, j: (i * nq + j, 0)),
            pl.BlockSpec((ncp, KV_W), lambda i, j: (i, 0)),
            pl.BlockSpec((4, t, HEAD_DIM), lambda i, j: (0, i, 0)),
            pl.BlockSpec((4, t, HEAD_DIM), lambda i, j: (0, i, 0)),
            pl.BlockSpec((ns, ncp), lambda i, j: (0, 0)),
            pl.BlockSpec((tq, tq), lambda i, j: (0, 0)),
            pl.BlockSpec((ns, t), lambda i, j: (0, 0)),
        ],
        out_specs=pl.BlockSpec((tq, D_MODEL), lambda i, j: (i * nq + j, 0)),
        out_shape=jax.ShapeDtypeStruct((b * t, D_MODEL), BF16),
        scratch_shapes=[
            pltpu.VMEM((1, rq * tq, 1), F32), pltpu.VMEM((1, rq * tq, 1), F32),
            pltpu.VMEM((1, rq * tq, HEAD_DIM), F32),
            pltpu.VMEM((t // tk, tq, tk), F32),
            pltpu.VMEM((ns, tq), F32),
        ],
        compiler_params=_cp(("parallel", "arbitrary")),
        name="nsa_attn_prompt",
    )(q, gates, kvc, slc16, wn16, mimp, eye, expand)


def _nsa_dec1_kernel(pt_ref, q_ref, pe_ref, wlo_ref, whi_ref, w2_ref, mimp_ref, exp_ref, *rest,
                     n_pages, td, past_len, ns_pad):
    pages = rest[:n_pages]
    ocmp_ref, mask_ref, sc_ref = rest[n_pages:]
    rq = NSA_HEADS // NSA_KV_HEADS
    n_real = -(-(past_len + td) // SEL_L)
    ch = jnp.concatenate([p[...] for p in pages], axis=0)
    kvc = _compress_rows(ch, pe_ref, wlo_ref, whi_ref, w2_ref).astype(BF16)
    qpos_col = past_len + _iota((td, 1), 0)
    pgs = []
    for g in range(NSA_KV_HEADS):
        q = q_ref[rq * g:rq * (g + 1)].reshape(rq * td, HEAD_DIM).astype(BF16)
        kc = kvc[:, HEAD_DIM * g:HEAD_DIM * (g + 1)]
        vc = kvc[:, HEAD_DIM * (NSA_KV_HEADS + g):HEAD_DIM * (NSA_KV_HEADS + g + 1)]
        o_cmp, pgrp = _cmp_branch(q, kc, vc, qpos_col, rq, td)
        for r in range(rq):
            h = rq * g + r
            ocmp_ref[:, HEAD_DIM * h:HEAD_DIM * (h + 1)] = o_cmp[r * td:(r + 1) * td]
        pgs.append(pgrp)
    pg_all = jnp.concatenate(pgs, axis=0)
    imp_t = _nt(mimp_ref[...], pg_all, lax.Precision.HIGHEST)
    qpos_row = past_len + (_iota((1, NSA_KV_HEADS * td), 1) % td)
    sel_t = _select_blocks(imp_t, qpos_row, sc_ref, n_real)
    c = NSA_KV_HEADS * td
    eye = (_iota((c, c), 0) == _iota((c, c), 1)).astype(BF16)
    sel = _nt(eye, sel_t.astype(BF16))
    mask_ref[...] = _mm(sel.astype(BF16), exp_ref[...]).astype(BF16)


def _nsa_dec1(page_table, q, pe, wlo, whi, w2, mimp, expand, cache_chunks, e, n_p, bd, td, ns_pad):
    n_pages = page_table.shape[1]
    past_len = n_pages * PAGE
    off = n_p // td
    ncp = past_len // CMP_D
    rows = PAGE // CMP_D
    full = lambda shape: pl.BlockSpec(shape, lambda b, pt: (0,) * len(shape))
    in_specs = [
        pl.BlockSpec((NSA_HEADS, td, HEAD_DIM), lambda b, pt: (0, off + b, 0)),
        full((2, CHUNK_W)), full((CHUNK_W, KV_W)), full((CHUNK_W, KV_W)), full((KV_W, KV_W)),
        full((ns_pad, ncp)), full((ns_pad, past_len)),
    ]
    in_specs += [pl.BlockSpec((None, None, rows, CHUNK_W), functools.partial(
        lambda b, pt, i: (e, pt[b, i], 0, 0), i=i)) for i in range(n_pages)]
    grid_spec = pltpu.PrefetchScalarGridSpec(
        num_scalar_prefetch=1,
        grid=(bd,),
        in_specs=in_specs,
        out_specs=[pl.BlockSpec((td, D_MODEL), lambda b, pt: (b, 0)),
                   pl.BlockSpec((None, NSA_KV_HEADS * td, past_len), lambda b, pt: (b, 0, 0))],
        scratch_shapes=[pltpu.VMEM((ns_pad, NSA_KV_HEADS * td), F32)],
    )
    return pl.pallas_call(
        functools.partial(_nsa_dec1_kernel, n_pages=n_pages, td=td, past_len=past_len, ns_pad=ns_pad),
        grid_spec=grid_spec,
        out_shape=[jax.ShapeDtypeStruct((bd * td, D_MODEL), F32),
                   jax.ShapeDtypeStruct((bd, NSA_KV_HEADS * td, past_len), BF16)],
        compiler_params=_cp(("parallel",)),
        name="nsa_decode_select",
    )(page_table, q, pe, wlo, whi, w2, mimp, expand, *([cache_chunks] * n_pages))


def _nsa_dec2_kernel(pt_ref, q_ref, gate_ref, ocmp_ref, mask_ref, slcn_ref, wnn_ref, state_ref, *rest,
                     pg, td, past_len):
    pages = rest[:pg]
    o_ref, nst_ref, m_ref, l_ref, acc_ref = rest[pg:]
    c = pl.program_id(1)
    rq = NSA_HEADS // NSA_KV_HEADS
    rows = rq * td
    wbuf = state_ref.shape[0]

    @pl.when(c == 0)
    def _():
        m_ref[...] = jnp.full_like(m_ref, NEG_INF)
        l_ref[...] = jnp.zeros_like(l_ref)
        acc_ref[...] = jnp.zeros_like(acc_ref)

    def update(g, s, vs):
        m_prev = m_ref[g]
        m_new = jnp.maximum(m_prev, jnp.max(s, axis=-1, keepdims=True))
        alpha = jnp.exp(m_prev - m_new)
        p = jnp.exp(s - m_new)
        l_ref[g] = alpha * l_ref[g] + jnp.sum(p, axis=-1, keepdims=True)
        w = vs[0].shape[0]
        pv = _mm(p[:, 0:w].astype(BF16), vs[0])
        for i in range(1, len(vs)):
            pv = pv + _mm(p[:, i * w:(i + 1) * w].astype(BF16), vs[i])
        acc_ref[g] = alpha * acc_ref[g] + pv
        m_ref[g] = m_new

    kvs = [pages[i][...].astype(BF16) for i in range(pg)]
    qgs = [q_ref[rq * g:rq * (g + 1)].reshape(rows, HEAD_DIM).astype(BF16) for g in range(NSA_KV_HEADS)]
    msk = mask_ref[...].astype(F32)
    for g in range(NSA_KV_HEADS):
        s = jnp.concatenate([_nt(qgs[g], kvs[i][:, HEAD_DIM * g:HEAD_DIM * (g + 1)]) for i in range(pg)], axis=1)
        keep = jnp.broadcast_to((msk[td * g:td * (g + 1)] > 0.5)[None], (rq, td, pg * PAGE)).reshape(rows, pg * PAGE)
        s = jnp.where(keep, s, NEG_INF)
        vo = (NSA_KV_HEADS + g) * HEAD_DIM
        update(g, s, [kvs[i][:, vo:vo + HEAD_DIM] for i in range(pg)])

    @pl.when(c == pl.num_programs(1) - 1)
    def _():
        gates = gate_ref[...]
        ocmp = ocmp_ref[...]
        slcn = slcn_ref[...].astype(BF16)
        wnn = wnn_ref[...]
        st = state_ref[...]
        nst_ref[0:wbuf - td, :] = st[td:wbuf, :]
        nst_ref[wbuf - td:wbuf, :] = wnn
        st16 = st.astype(BF16)
        wnn16 = wnn.astype(BF16)
        t_row = _iota((rows, td), 0) % td
        keep_new = _iota((rows, td), 1) <= t_row
        t_row_w = _iota((rows, wbuf), 0) % td
        dpos = (past_len + t_row_w) - (past_len - wbuf + _iota((rows, wbuf), 1))
        keep_st = (dpos >= 0) & (dpos < WINDOW)
        for g in range(NSA_KV_HEADS):
            ko, vo = HEAD_DIM * g, HEAD_DIM * (NSA_KV_HEADS + g)
            s = jnp.where(keep_new, _nt(qgs[g], slcn[:, ko:ko + HEAD_DIM]), NEG_INF)
            update(g, s, [slcn[:, vo:vo + HEAD_DIM]])
            o_slc = acc_ref[g] / l_ref[g]
            s1 = jnp.where(keep_st, _nt(qgs[g], st16[:, ko:ko + HEAD_DIM]), NEG_INF)
            s2 = jnp.where(keep_new, _nt(qgs[g], wnn16[:, ko:ko + HEAD_DIM]), NEG_INF)
            mx = jnp.maximum(jnp.max(s1, axis=-1, keepdims=True), jnp.max(s2, axis=-1, keepdims=True))
            p1 = jnp.exp(s1 - mx)
            p2 = jnp.exp(s2 - mx)
            den = jnp.sum(p1, axis=-1, keepdims=True) + jnp.sum(p2, axis=-1, keepdims=True)
            o_win = (_mm(p1.astype(BF16), st16[:, vo:vo + HEAD_DIM])
                     + _mm(p2.astype(BF16), wnn16[:, vo:vo + HEAD_DIM])) / den
            for r in range(rq):
                h = rq * g + r
                rs = slice(r * td, (r + 1) * td)
                cols = slice(HEAD_DIM * h, HEAD_DIM * (h + 1))
                o_ref[:, cols] = (gates[:, h:h + 1] * ocmp[:, cols]
                                  + gates[:, NSA_HEADS + h:NSA_HEADS + h + 1] * o_slc[rs]
                                  + gates[:, 2 * NSA_HEADS + h:2 * NSA_HEADS + h + 1] * o_win[rs])


def _nsa_dec2(page_table, q, gates, ocmp, mask, slc32, wn32, state, cache_slc, e, n_p, bd, td, pg):
    n_pages = page_table.shape[1]
    past_len = n_pages * PAGE
    off = n_p // td
    rq = NSA_HEADS // NSA_KV_HEADS
    wbuf = state.shape[2]
    in_specs = [
        pl.BlockSpec((NSA_HEADS, td, HEAD_DIM), lambda b, c, pt: (0, off + b, 0)),
        pl.BlockSpec((td, N_GATES), lambda b, c, pt: (off + b, 0)),
        pl.BlockSpec((td, D_MODEL), lambda b, c, pt: (b, 0)),
        pl.BlockSpec((None, NSA_KV_HEADS * td, pg * PAGE), lambda b, c, pt: (b, 0, c)),
        pl.BlockSpec((td, KV_W), lambda b, c, pt: (off + b, 0)),
        pl.BlockSpec((td, KV_W), lambda b, c, pt: (off + b, 0)),
        pl.BlockSpec((None, None, wbuf, KV_W), lambda b, c, pt: (e, b, 0, 0)),
    ]
    in_specs += [pl.BlockSpec((None, None, PAGE, KV_W), functools.partial(
        lambda b, c, pt, i: (e, pt[b, c * pg + i], 0, 0), i=i)) for i in range(pg)]
    grid_spec = pltpu.PrefetchScalarGridSpec(
        num_scalar_prefetch=1,
        grid=(bd, n_pages // pg),
        in_specs=in_specs,
        out_specs=[pl.BlockSpec((td, D_MODEL), lambda b, c, pt: (b, 0)),
                   pl.BlockSpec((None, wbuf, KV_W), lambda b, c, pt: (b, 0, 0))],
        scratch_shapes=[
            pltpu.VMEM((NSA_KV_HEADS, rq * td, 1), F32), pltpu.VMEM((NSA_KV_HEADS, rq * td, 1), F32),
            pltpu.VMEM((NSA_KV_HEADS, rq * td, HEAD_DIM), F32),
        ],
    )
    return pl.pallas_call(
        functools.partial(_nsa_dec2_kernel, pg=pg, td=td, past_len=past_len),
        grid_spec=grid_spec,
        out_shape=[jax.ShapeDtypeStruct((bd * td, D_MODEL), F32),
                   jax.ShapeDtypeStruct((bd, wbuf, KV_W), F32)],
        compiler_params=_cp(("parallel", "arbitrary")),
        name="nsa_decode_attend",
    )(page_table, q, gates, ocmp, mask, slc32, wn32, state, *([cache_slc] * pg))


def _prep_even(w_in, b_f, w_q_up, w_uk, w_uv):
    pad = jnp.zeros((D_MODEL, EV_COLS - 1192), F32)
    w = jnp.concatenate([w_in[:, 0:768], w_in[:, 776:1192], w_in[:, 768:776], pad], axis=1).astype(BF16)
    bf = jnp.zeros((1, LANES), F32).at[0, EV_FLG_LANE:EV_FLG_LANE + FOX_HEADS].set(b_f)
    wq = w_q_up.reshape(MLA_Q_LORA, MLA_HEADS, MLA_QK)
    wq = jnp.concatenate([wq[:, :, :MLA_NOPE].reshape(MLA_Q_LORA, -1),
                          wq[:, :, MLA_NOPE:].reshape(MLA_Q_LORA, -1)], axis=1).astype(BF16)
    wuk_h = jnp.transpose(w_uk, (1, 2, 0))
    eye = jnp.eye(MLA_HEADS, dtype=F32)
    wuk_bd = (eye[:, None, :, None] * wuk_h[:, :, None, :]).reshape(MLA_HEADS * MLA_NOPE, MLA_HEADS * MLA_KV_LORA)
    wuv = jnp.transpose(w_uv, (1, 0, 2)).astype(BF16)
    return w, bf, wq, wuk_bd.astype(BF16), wuv


def _prep_odd(w_in, b_gate, pe, w1, w2):
    qw = NSA_HEADS * HEAD_DIM
    seg = lambda n: w_in[:, qw + LANES * n:qw + LANES * (n + 1)]
    gcol = w_in[:, qw + 6 * LANES:qw + 6 * LANES + N_GATES]
    perm = np.array([h * 3 + c for c in range(3) for h in range(NSA_HEADS)])
    pad = jnp.zeros((D_MODEL, LANES - N_GATES), F32)
    w = jnp.concatenate([w_in[:, :qw], seg(0), seg(2), seg(4), seg(1), seg(3), seg(5), gcol[:, perm], pad],
                        axis=1).astype(BF16)
    bg = jnp.zeros((1, LANES), F32).at[0, 0:N_GATES].set(b_gate[perm])
    kv_of = (0, 0, 1, 1)
    pe_rows, w_halves = [], []
    eye4 = jnp.eye(4, dtype=F32)
    for half in range(2):
        ls = slice(half * CMP_D, (half + 1) * CMP_D)
        pe_rows.append(jnp.stack([pe[kv_of[j], ls, :] for j in range(4)], axis=1).reshape(CHUNK_W))
        w1r = jnp.stack([w1[kv_of[j]].reshape(CMP_L, HEAD_DIM, HEAD_DIM)[ls] for j in range(4)], axis=1)
        wexp = (w1r[:, :, :, None, :] * eye4[None, :, None, :, None]).reshape(CHUNK_W, KV_W)
        w_halves.append(wexp.astype(BF16))
    w2bd = (eye4[:, None, :, None] * jnp.stack([w2[kv_of[j]] for j in range(4)])[:, :, None, :]).reshape(KV_W, KV_W)
    return w, bg, jnp.stack(pe_rows), w_halves[0], w_halves[1], w2bd.astype(BF16)


def kernel(x_prompt, x_sample, cache_fox_kv, cache_fox_logf, cache_mla, cache_nsa_cmp, cache_nsa_slc,
           state_nsa_win, page_table, norm_w, final_norm, ffn_w_gate, ffn_w_up, ffn_w_down,
           ev_w_in, ev_b_f, mla_q_norm, mla_w_q_up, mla_kv_norm, mla_w_uk, mla_w_uv, ev_w_out,
           od_w_in, od_b_gate, nsa_cmp_pe, nsa_cmp_w1, nsa_cmp_w2, od_w_out):
    b, t, _ = x_prompt.shape
    bd, td, _ = x_sample.shape
    depth = norm_w.shape[0]
    n_pages = page_table.shape[1]
    past_len = n_pages * PAGE
    n_p, n_s = b * t, bd * td
    n = n_p + n_s
    n_pool = cache_fox_kv.shape[1]
    wbuf = state_nsa_win.shape[2]

    tm = math.gcd(512, n_s)
    tf = 256
    tq = tk = min(256, t)
    pg = min(8, n_pages)
    ns_pad = -(-(-(-(past_len + td) // SEL_L)) // 16) * 16

    rope_mla = _rope_tables(t, past_len, td, tm, MLA_THETA, MLA_ROPE, MLA_ROPE)
    rope_nsa = _rope_tables(t, past_len, td, tm, ROPE_THETA, ROPE_DIM, HEAD_DIM)
    mimp_p = _imp_matrix(t // SEL_L, t // CMP_D)
    mimp_s = _imp_matrix(ns_pad, past_len // CMP_D)
    expand_p = _expand_matrix(t // SEL_L, t)
    expand_s = _expand_matrix(ns_pad, past_len)
    eye_q = jnp.eye(tq, dtype=BF16)

    wg16, wu16, wd16 = ffn_w_gate.astype(BF16), ffn_w_up.astype(BF16), ffn_w_down.astype(BF16)
    cache_kv2 = cache_fox_kv.reshape(cache_fox_kv.shape[0], n_pool, PAGE, KV_W)
    cache_lft = jnp.swapaxes(cache_fox_logf, 2, 3)
    cache_cmp2 = cache_nsa_cmp.reshape(cache_nsa_cmp.shape[0], n_pool, PAGE // CMP_D, CHUNK_W)
    cache_slc2 = cache_nsa_slc.reshape(cache_nsa_slc.shape[0], n_pool, PAGE, KV_W)
    state2 = state_nsa_win.reshape(state_nsa_win.shape[0], bd, wbuf, KV_W)

    h = jnp.concatenate([x_prompt.reshape(n_p, D_MODEL), x_sample.reshape(n_s, D_MODEL)], axis=0)
    row = lambda v: v.reshape(1, -1)
    kv_shape = (2, FOX_KV_HEADS, HEAD_DIM)
    outs = {k: [] for k in ("fkv_p", "fkv_s", "flf_p", "flf_s", "mla_p", "mla_s",
                            "cmp_p", "cmp_s", "slc_p", "slc_s", "win_p", "win_s")}
    for li in range(depth):
        e = li // 2
        h = _ffn(h, row(norm_w[li, 0]), wg16[li, 0], wu16[li, 0], wd16[li, 0], tm, tf)
        if li % 2 == 0:
            w, bf, wq, wuk_bd, wuv = _prep_even(ev_w_in[e], ev_b_f[e], mla_w_q_up[e], mla_w_uk[e], mla_w_uv[e])
            qf, kv32, kv16, logf, qm, lat32, lat16 = _even_proj(
                h, row(norm_w[li, 1]), w, bf, row(mla_q_norm[e]), wq, wuk_bd, row(mla_kv_norm[e]),
                rope_mla, tm, t, n_p)
            csum, cst = _csum(logf, b, t, tk)
            o_p = _even_attn(qf, qm, csum, kv16, lat16, cst, wuv, b, t, tq, tk)
            o_s = _even_dec(page_table, qf, qm, kv32, lat32, logf, wuv, cache_kv2, cache_lft, cache_mla,
                            e, n_p, bd, td, pg)
            h = _out_proj(h, o_p, o_s, ev_w_out[e].astype(BF16), tm)
            outs["fkv_p"].append(kv32[:n_p].reshape((b, t) + kv_shape))
            outs["fkv_s"].append(kv32[n_p:].reshape((bd, td) + kv_shape))
            outs["flf_p"].append(logf[:n_p].reshape(b, t, FOX_HEADS))
            outs["flf_s"].append(logf[n_p:].reshape(bd, td, FOX_HEADS))
            outs["mla_p"].append(lat32[:n_p].reshape(b, t, MLA_LAT))
            outs["mla_s"].append(lat32[n_p:].reshape(bd, td, MLA_LAT))
        else:
            w, bg, pe2, wlo, whi, w2bd = _prep_odd(od_w_in[e], od_b_gate[e], nsa_cmp_pe[e], nsa_cmp_w1[e],
                                                   nsa_cmp_w2[e])
            q, cmp32, slc32, wn32, slc16, wn16, gates = _odd_proj(
                h, row(norm_w[li, 1]), w, bg, rope_nsa, tm, t, n_p)
            kvc = _compress_prompt(cmp32[:n_p].reshape(n_p // CMP_D, CHUNK_W), pe2, wlo, whi, w2bd,
                                   b, t // CMP_D)
            o_p = _nsa_prompt(q, gates, kvc, slc16, wn16, mimp_p, eye_q, expand_p, b, t, tq, tk)
            ocmp, mask = _nsa_dec1(page_table, q, pe2, wlo, whi, w2bd, mimp_s, expand_s, cache_cmp2,
                                   e, n_p, bd, td, ns_pad)
            o_s, nst = _nsa_dec2(page_table, q, gates, ocmp, mask, slc32, wn32, state2, cache_slc2,
                                 e, n_p, bd, td, pg)
            h = _out_proj(h, o_p, o_s, od_w_out[e].astype(BF16), tm)
            wlen = min(WINDOW, t)
            outs["cmp_p"].append(cmp32[:n_p].reshape((b, t) + kv_shape))
            outs["cmp_s"].append(cmp32[n_p:].reshape((bd, td) + kv_shape))
            outs["slc_p"].append(slc32[:n_p].reshape((b, t) + kv_shape))
            outs["slc_s"].append(slc32[n_p:].reshape((bd, td) + kv_shape))
            outs["win_p"].append(wn32[:n_p].reshape((b, t) + kv_shape)[:, t - wlen:])
            outs["win_s"].append(nst.reshape((bd, wbuf) + kv_shape))
        h = _ffn(h, row(norm_w[li, 2]), wg16[li, 1], wu16[li, 1], wd16[li, 1], tm, tf)
    y = _final_norm(h, row(final_norm), tm)
    st = lambda k: jnp.stack(outs[k])
    return (y[:n_p].reshape(b, t, D_MODEL), y[n_p:].reshape(bd, td, D_MODEL),
            st("fkv_p"), st("fkv_s"), st("flf_p"), st("flf_s"), st("mla_p"), st("mla_s"),
            st("cmp_p"), st("cmp_s"), st("slc_p"), st("slc_s"), st("win_p"), st("win_s"))
```

```python
import functools
import math

import numpy as np
import jax
import jax.numpy as jnp
from jax import lax
from jax.experimental import pallas as pl
from jax.experimental.pallas import tpu as pltpu

F32 = jnp.float32
BF16 = jnp.bfloat16

D_MODEL = 1024
HEAD_DIM = 64
FOX_HEADS = 8
FOX_KV_HEADS = 2
MLA_HEADS = 8
MLA_Q_LORA = 256
MLA_KV_LORA = 128
MLA_NOPE = 64
MLA_ROPE = 32
MLA_V = 64
MLA_THETA = 10000.0
NSA_HEADS = 16
NSA_KV_HEADS = 2
CMP_L = 32
CMP_D = 16
SEL_L = 64
TOP_N = 16
WINDOW = 512
ROPE_THETA = 500000.0
ROPE_DIM = HEAD_DIM // 4
FF_DIM = 2816
EPS = 1e-6
FORCE_SCORE = 1e9
NEG_INF = -1e30
PAGE = 128

LANES = 128
MLA_LAT = MLA_KV_LORA + MLA_ROPE
MLA_QK = MLA_NOPE + MLA_ROPE
FOX_SCALE = HEAD_DIM ** -0.5
MLA_SCALE = MLA_QK ** -0.5
KV_W = 2 * FOX_KV_HEADS * HEAD_DIM
CHUNK_W = CMP_D * KV_W
VMEM_LIMIT = 56 * 1024 * 1024


def _cp(sem):
    return pltpu.CompilerParams(dimension_semantics=sem, vmem_limit_bytes=VMEM_LIMIT)


def _nt(a, b, precision=None):
    return lax.dot_general(a, b, (((1,), (1,)), ((), ())), precision=precision,
                           preferred_element_type=F32)


def _mm(a, b, precision=None):
    return jnp.dot(a, b, precision=precision, preferred_element_type=F32)


def _rms_val(x, g):
    return (x * lax.rsqrt(jnp.mean(x * x, axis=-1, keepdims=True) + EPS)) * g


def _log_sigmoid(x):
    return -(jnp.maximum(-x, 0.0) + jnp.log1p(jnp.exp(-jnp.abs(x))))


def _gelu_tanh(x):
    return x * (0.5 * (1.0 + jnp.tanh(math.sqrt(2.0 / math.pi) * (x + 0.044715 * (x * x * x)))))


def _iota(shape, dim):
    return lax.broadcasted_iota(jnp.int32, shape, dim)


def _ffn_kernel(x_ref, g_ref, wg_ref, wu_ref, wd_ref, o_ref, xn_ref, acc_ref):
    j = pl.program_id(1)

    @pl.when(j == 0)
    def _():
        xn_ref[...] = _rms_val(x_ref[...], g_ref[...]).astype(BF16)
        acc_ref[...] = jnp.zeros_like(acc_ref)

    xn = xn_ref[...]
    g = _mm(xn, wg_ref[...])
    u = _mm(xn, wu_ref[...])
    a = (g * jax.nn.sigmoid(g)) * u
    acc_ref[...] += _mm(a.astype(BF16), wd_ref[...])

    @pl.when(j == pl.num_programs(1) - 1)
    def _():
        o_ref[...] = x_ref[...] + 0.5 * acc_ref[...]


def _ffn(h, g, wg, wu, wd, tm, tf):
    n = h.shape[0]
    return pl.pallas_call(
        _ffn_kernel,
        grid=(n // tm, FF_DIM // tf),
        in_specs=[
            pl.BlockSpec((tm, D_MODEL), lambda i, j: (i, 0)),
            pl.BlockSpec((1, D_MODEL), lambda i, j: (0, 0)),
            pl.BlockSpec((D_MODEL, tf), lambda i, j: (0, j)),
            pl.BlockSpec((D_MODEL, tf), lambda i, j: (0, j)),
            pl.BlockSpec((tf, D_MODEL), lambda i, j: (j, 0)),
        ],
        out_specs=pl.BlockSpec((tm, D_MODEL), lambda i, j: (i, 0)),
        out_shape=jax.ShapeDtypeStruct((n, D_MODEL), F32),
        scratch_shapes=[pltpu.VMEM((tm, D_MODEL), BF16), pltpu.VMEM((tm, D_MODEL), F32)],
        compiler_params=_cp(("parallel", "arbitrary")),
        name="ffn",
    )(h, g, wg, wu, wd)


def _out_proj_kernel(h_ref, op_ref, os_ref, w_ref, o_ref, *, npt):
    i = pl.program_id(0)

    @pl.when(i < npt)
    def _():
        o_ref[...] = h_ref[...] + _mm(op_ref[...], w_ref[...])

    @pl.when(i >= npt)
    def _():
        o_ref[...] = h_ref[...] + _mm(os_ref[...].astype(BF16), w_ref[...])


def _out_proj(h, o_p, o_s, w, tm):
    n = h.shape[0]
    npt = o_p.shape[0] // tm
    return pl.pallas_call(
        functools.partial(_out_proj_kernel, npt=npt),
        grid=(n // tm,),
        in_specs=[
            pl.BlockSpec((tm, D_MODEL), lambda i: (i, 0)),
            pl.BlockSpec((tm, D_MODEL), lambda i: (jnp.minimum(i, npt - 1), 0)),
            pl.BlockSpec((tm, D_MODEL), lambda i: (jnp.maximum(i - npt, 0), 0)),
            pl.BlockSpec((D_MODEL, D_MODEL), lambda i: (0, 0)),
        ],
        out_specs=pl.BlockSpec((tm, D_MODEL), lambda i: (i, 0)),
        out_shape=jax.ShapeDtypeStruct((n, D_MODEL), F32),
        compiler_params=_cp(("parallel",)),
        name="out_proj",
    )(h, o_p, o_s, w)


def _final_norm_kernel(x_ref, g_ref, o_ref):
    o_ref[...] = _rms_val(x_ref[...], g_ref[...])


def _final_norm(h, g, tm):
    n = h.shape[0]
    return pl.pallas_call(
        _final_norm_kernel,
        grid=(n // tm,),
        in_specs=[pl.BlockSpec((tm, D_MODEL), lambda i: (i, 0)),
                  pl.BlockSpec((1, D_MODEL), lambda i: (0, 0))],
        out_specs=pl.BlockSpec((tm, D_MODEL), lambda i: (i, 0)),
        out_shape=jax.ShapeDtypeStruct((n, D_MODEL), F32),
        compiler_params=_cp(("parallel",)),
        name="final_norm",
    )(h, g)


def _rope_tables(t_prompt, past_len, t_dec, tm, theta, rot_dim, period):
    half = rot_dim // 2
    pos = np.concatenate([np.arange(t_prompt), past_len + (np.arange(tm) % t_dec)]).astype(np.float64)
    inv = np.float64(theta) ** (-np.arange(half, dtype=np.float64) / half)
    ang = pos[:, None] * inv[None, :]
    cos, sin = np.cos(ang), np.sin(ang)
    lane = np.arange(LANES) % period
    c = np.ones((pos.shape[0], LANES))
    s1 = np.zeros((pos.shape[0], LANES))
    s2 = np.zeros((pos.shape[0], LANES))
    lo = lane < half
    hi = (lane >= half) & (lane < rot_dim)
    c[:, lo] = cos[:, lane[lo]]
    c[:, hi] = cos[:, lane[hi] - half]
    s1[:, hi] = sin[:, lane[hi] - half]
    s2[:, lo] = -sin[:, lane[lo]]
    return jnp.asarray(np.stack([c, s1, s2]), dtype=F32)


def _rope_chunk(x, rope_ref, half):
    return (x * rope_ref[0] + pltpu.roll(x, half, 1) * rope_ref[1]
            + pltpu.roll(x, LANES - half, 1) * rope_ref[2])


def _rope_spec(tm, t_prompt, npt):
    tiles = t_prompt // tm
    return pl.BlockSpec((3, tm, LANES), lambda i: (0, jnp.where(i < npt, i % tiles, tiles), 0))


EV_COLS = 1280
EV_FLG_LANE = 32


def _even_proj_kernel(x_ref, g_ref, win_ref, bf_ref, qn_ref, wqu_ref, wuk_ref, kvn_ref, rope_ref,
                      qf_ref, kv32_ref, kv16_ref, logf_ref, qm_ref, lat32_ref, lat16_ref):
    xn = _rms_val(x_ref[...], g_ref[...]).astype(BF16)
    z = _mm(xn, win_ref[...])
    for h in range(FOX_HEADS):
        qf_ref[h] = z[:, HEAD_DIM * h:HEAD_DIM * (h + 1)] * FOX_SCALE
    kv = z[:, 512:768]
    kv32_ref[...] = kv
    for j in range(4):
        kv16_ref[j] = kv[:, HEAD_DIM * j:HEAD_DIM * (j + 1)].astype(BF16)
    cq = _rms_val(z[:, 768:1024], qn_ref[...]).astype(BF16)
    qmm = _mm(cq, wqu_ref[...])
    qlat = _mm(qmm[:, :512].astype(BF16), wuk_ref[...]) * MLA_SCALE
    for h in range(MLA_HEADS):
        qm_ref[h, :, 0:MLA_KV_LORA] = qlat[:, MLA_KV_LORA * h:MLA_KV_LORA * (h + 1)]
    for c in range(2):
        pe = _rope_chunk(qmm[:, 512 + LANES * c:512 + LANES * (c + 1)], rope_ref, MLA_ROPE // 2) * MLA_SCALE
        for hh in range(4):
            qm_ref[4 * c + hh, :, MLA_KV_LORA:MLA_LAT] = pe[:, MLA_ROPE * hh:MLA_ROPE * (hh + 1)]
    ckv = _rms_val(z[:, 1024:1152], kvn_ref[...])
    last = z[:, 1152:1280]
    kpe = _rope_chunk(last, rope_ref, MLA_ROPE // 2)[:, 0:MLA_ROPE]
    lat32_ref[:, 0:MLA_KV_LORA] = ckv
    lat32_ref[:, MLA_KV_LORA:MLA_LAT] = kpe
    lat16_ref[:, 0:MLA_KV_LORA] = ckv.astype(BF16)
    lat16_ref[:, MLA_KV_LORA:MLA_LAT] = kpe.astype(BF16)
    logf_ref[...] = _log_sigmoid(last + bf_ref[...])[:, EV_FLG_LANE:EV_FLG_LANE + FOX_HEADS]


def _even_proj(h, g, w_in, b_f, q_norm, w_q_up, wuk_bd, kv_norm, rope, tm, t_prompt, n_p):
    n = h.shape[0]
    npt = n_p // tm
    full = lambda shape: pl.BlockSpec(shape, lambda i: (0,) * len(shape))
    return pl.pallas_call(
        _even_proj_kernel,
        grid=(n // tm,),
        in_specs=[
            pl.BlockSpec((tm, D_MODEL), lambda i: (i, 0)),
            full((1, D_MODEL)), full((D_MODEL, EV_COLS)), full((1, LANES)), full((1, MLA_Q_LORA)),
            full((MLA_Q_LORA, 768)), full((512, 1024)), full((1, MLA_KV_LORA)),
            _rope_spec(tm, t_prompt, npt),
        ],
        out_specs=[
            pl.BlockSpec((FOX_HEADS, tm, HEAD_DIM), lambda i: (0, i, 0)),
            pl.BlockSpec((tm, KV_W), lambda i: (i, 0)),
            pl.BlockSpec((4, tm, HEAD_DIM), lambda i: (0, i, 0)),
            pl.BlockSpec((tm, FOX_HEADS), lambda i: (i, 0)),
            pl.BlockSpec((MLA_HEADS, tm, MLA_LAT), lambda i: (0, i, 0)),
            pl.BlockSpec((tm, MLA_LAT), lambda i: (i, 0)),
            pl.BlockSpec((tm, MLA_LAT), lambda i: (i, 0)),
        ],
        out_shape=[
            jax.ShapeDtypeStruct((FOX_HEADS, n, HEAD_DIM), F32),
            jax.ShapeDtypeStruct((n, KV_W), F32),
            jax.ShapeDtypeStruct((4, n, HEAD_DIM), BF16),
            jax.ShapeDtypeStruct((n, FOX_HEADS), F32),
            jax.ShapeDtypeStruct((MLA_HEADS, n, MLA_LAT), F32),
            jax.ShapeDtypeStruct((n, MLA_LAT), F32),
            jax.ShapeDtypeStruct((n, MLA_LAT), BF16),
        ],
        compiler_params=_cp(("parallel",)),
        name="even_proj",
    )(h, g, w_in, b_f, q_norm, w_q_up, wuk_bd, kv_norm, rope)


CS_CHUNK = 128


def _csum_kernel(lf_ref, cs_ref, cst_ref, *, t, tk):
    r = _iota((CS_CHUNK, CS_CHUNK), 0)
    c = _iota((CS_CHUNK, CS_CHUNK), 1)
    tri = (c <= r).astype(F32)
    eye = (_iota((FOX_HEADS, FOX_HEADS), 0) == _iota((FOX_HEADS, FOX_HEADS), 1)).astype(F32)
    carry = jnp.zeros((1, FOX_HEADS), F32)
    per = tk // CS_CHUNK
    for k in range(t // CS_CHUNK):
        lf = lf_ref[k * CS_CHUNK:(k + 1) * CS_CHUNK, :]
        cs = _mm(tri, lf, lax.Precision.HIGHEST) + carry
        carry = cs[CS_CHUNK - 1:CS_CHUNK, :]
        cs_ref[k * CS_CHUNK:(k + 1) * CS_CHUNK, :] = cs
        cst_ref[k // per, :, (k % per) * CS_CHUNK:(k % per + 1) * CS_CHUNK] = _nt(eye, cs, lax.Precision.HIGHEST)


def _csum(logf, b, t, tk):
    return pl.pallas_call(
        functools.partial(_csum_kernel, t=t, tk=tk),
        grid=(b,),
        in_specs=[pl.BlockSpec((t, FOX_HEADS), lambda i: (i, 0))],
        out_specs=[pl.BlockSpec((t, FOX_HEADS), lambda i: (i, 0)),
                   pl.BlockSpec((None, t // tk, FOX_HEADS, tk), lambda i: (i, 0, 0, 0))],
        out_shape=[jax.ShapeDtypeStruct((b * t, FOX_HEADS), F32),
                   jax.ShapeDtypeStruct((b, t // tk, FOX_HEADS, tk), F32)],
        compiler_params=_cp(("parallel",)),
        name="fox_csum",
    )(logf)


def _lane_chunks(s):
    return [s[:, LANES * c:LANES * (c + 1)] for c in range(s.shape[1] // LANES)]


def _online_update(chunks, m_ref, l_ref, acc_ref, v, idx):
    m_prev = m_ref[idx]
    mx = chunks[0]
    for c in chunks[1:]:
        mx = jnp.maximum(mx, c)
    m_new = jnp.maximum(m_prev, jnp.broadcast_to(jnp.max(mx, axis=-1, keepdims=True), m_prev.shape))
    alpha = jnp.exp(m_prev - m_new)
    ps = [jnp.exp(c - m_new) for c in chunks]
    lsum = ps[0]
    for p in ps[1:]:
        lsum = lsum + p
    l_ref[idx] = alpha * l_ref[idx] + lsum
    dv = acc_ref.shape[-1]
    acc_ref[idx] = alpha[:, :dv] * acc_ref[idx] + _mm(jnp.concatenate(ps, axis=1).astype(BF16), v)
    m_ref[idx] = m_new


def _online_finish(l_ref, acc_ref, idx):
    return acc_ref[idx] / jnp.sum(l_ref[idx], axis=-1, keepdims=True)


def _even_attn_kernel(qf_ref, qm_ref, cs_ref, kv_ref, lat_ref, cst_ref, wuv_ref, o_ref,
                      mf_ref, lf_ref, af_ref, mm_ref, lm_ref, am_ref, *, tq, tk):
    qi = pl.program_id(1)
    qs = qi * tq
    rq = FOX_HEADS // FOX_KV_HEADS
    nch = tk // LANES
    mf_ref[...] = jnp.full_like(mf_ref, NEG_INF)
    lf_ref[...] = jnp.zeros_like(lf_ref)
    af_ref[...] = jnp.zeros_like(af_ref)
    mm_ref[...] = jnp.full_like(mm_ref, NEG_INF)
    lm_ref[...] = jnp.zeros_like(lm_ref)
    am_ref[...] = jnp.zeros_like(am_ref)
    cs = cs_ref[...]
    cqb = [jnp.broadcast_to(cs[:, h:h + 1], (tq, LANES)) for h in range(FOX_HEADS)]
    qg = [qf_ref[rq * g:rq * (g + 1)].reshape(rq * tq, HEAD_DIM).astype(BF16) for g in range(FOX_KV_HEADS)]
    qm = qm_ref[...].reshape(MLA_HEADS * tq, MLA_LAT).astype(BF16)
    qpos = qs + _iota((tq, LANES), 0)

    def step(j, masked):
        ks = pl.multiple_of(j * tk, tk)
        if masked:
            keep = [(ks + LANES * c + _iota((tq, LANES), 1)) <= qpos for c in range(nch)]
        ck = cst_ref[j]
        for g in range(FOX_KV_HEADS):
            k = kv_ref[g, pl.ds(ks, tk), :]
            v = kv_ref[FOX_KV_HEADS + g, pl.ds(ks, tk), :]
            s = _nt(qg[g], k)
            chunks = []
            for c in range(nch):
                parts = []
                for r in range(rq):
                    h = rq * g + r
                    sr = s[r * tq:(r + 1) * tq, LANES * c:LANES * (c + 1)] + (
                        cqb[h] - ck[h:h + 1, LANES * c:LANES * (c + 1)])
                    parts.append(jnp.where(keep[c], sr, NEG_INF) if masked else sr)
                chunks.append(jnp.concatenate(parts, axis=0))
            _online_update(chunks, mf_ref, lf_ref, af_ref, v, g)
        lat = lat_ref[pl.ds(ks, tk), :]
        chunks = _lane_chunks(_nt(qm, lat))
        if masked:
            chunks = [jnp.where(keep[c][None], chunks[c].reshape(MLA_HEADS, tq, LANES), NEG_INF)
                      .reshape(MLA_HEADS * tq, LANES) for c in range(nch)]
        _online_update(chunks, mm_ref, lm_ref, am_ref, lat[:, :MLA_KV_LORA], 0)

    n_full = qs // tk
    n_all = (qs + tq - 1) // tk + 1

    def full_body(j, c):
        step(j, False)
        return c

    def diag_body(j, c):
        step(j, True)
        return c

    lax.fori_loop(0, n_full, full_body, 0)
    lax.fori_loop(n_full, n_all, diag_body, 0)

    for g in range(FOX_KV_HEADS):
        o = _online_finish(lf_ref, af_ref, g)
        for r in range(rq):
            h = rq * g + r
            o_ref[:, HEAD_DIM * h:HEAD_DIM * (h + 1)] = o[r * tq:(r + 1) * tq].astype(BF16)
    ol = _online_finish(lm_ref, am_ref, 0).astype(BF16)
    base = FOX_HEADS * HEAD_DIM
    for h in range(MLA_HEADS):
        om = _mm(ol[h * tq:(h + 1) * tq], wuv_ref[h])
        o_ref[:, base + MLA_V * h:base + MLA_V * (h + 1)] = om.astype(BF16)


def _even_attn(qf, qm, csum, kv16, lat16, cst, wuv, b, t, tq, tk):
    nq = t // tq
    rq = FOX_HEADS // FOX_KV_HEADS
    return pl.pallas_call(
        functools.partial(_even_attn_kernel, tq=tq, tk=tk),
        grid=(b, nq),
        in_specs=[
            pl.BlockSpec((FOX_HEADS, tq, HEAD_DIM), lambda i, j: (0, i * nq + j, 0)),
            pl.BlockSpec((MLA_HEADS, tq, MLA_LAT), lambda i, j: (0, i * nq + j, 0)),
            pl.BlockSpec((tq, FOX_HEADS), lambda i, j: (i * nq + j, 0)),
            pl.BlockSpec((4, t, HEAD_DIM), lambda i, j: (0, i, 0)),
            pl.BlockSpec((t, MLA_LAT), lambda i, j: (i, 0)),
            pl.BlockSpec((None, t // tk, FOX_HEADS, tk), lambda i, j: (i, 0, 0, 0)),
            pl.BlockSpec((MLA_HEADS, MLA_KV_LORA, MLA_V), lambda i, j: (0, 0, 0)),
        ],
        out_specs=pl.BlockSpec((tq, D_MODEL), lambda i, j: (i * nq + j, 0)),
        out_shape=jax.ShapeDtypeStruct((b * t, D_MODEL), BF16),
        scratch_shapes=[
            pltpu.VMEM((FOX_KV_HEADS, rq * tq, LANES), F32), pltpu.VMEM((FOX_KV_HEADS, rq * tq, LANES), F32),
            pltpu.VMEM((FOX_KV_HEADS, rq * tq, HEAD_DIM), F32),
            pltpu.VMEM((1, MLA_HEADS * tq, LANES), F32), pltpu.VMEM((1, MLA_HEADS * tq, LANES), F32),
            pltpu.VMEM((1, MLA_HEADS * tq, MLA_KV_LORA), F32),
        ],
        compiler_params=_cp(("parallel", "arbitrary")),
        name="even_attn_prompt",
    )(qf, qm, csum, kv16, lat16, cst, wuv)


def _lane_cumsum(x):
    lane = _iota(x.shape, 1)
    d = 1
    while d < LANES:
        x = x + jnp.where(lane >= d, pltpu.roll(x, d, 1), 0.0)
        d *= 2
    return x


def _dec_update(m_ref, l_ref, acc_ref, idx, s, pv_fn):
    m_prev = m_ref[idx]
    m_new = jnp.maximum(m_prev, jnp.max(s, axis=-1, keepdims=True))
    alpha = jnp.exp(m_prev - m_new)
    p = jnp.exp(s - m_new)
    l_ref[idx] = alpha * l_ref[idx] + jnp.sum(p, axis=-1, keepdims=True)
    acc_ref[idx] = alpha * acc_ref[idx] + pv_fn(p.astype(BF16))
    m_ref[idx] = m_new


def _even_dec_kernel(pt_ref, qf_ref, qm_ref, kvn_ref, latn_ref, lfn_ref, wuv_ref, *rest, pg, td):
    kv_refs = rest[:pg]
    lf_refs = rest[pg:2 * pg]
    lat_refs = rest[2 * pg:3 * pg]
    o_ref = rest[3 * pg]
    mf_ref, lf_ref, af_ref, mm_ref, lm_ref, am_ref, car_ref = rest[3 * pg + 1:]
    c = pl.program_id(1)
    rq = FOX_HEADS // FOX_KV_HEADS
    rows = rq * td

    @pl.when(c == 0)
    def _():
        mf_ref[...] = jnp.full_like(mf_ref, NEG_INF)
        lf_ref[...] = jnp.zeros_like(lf_ref)
        af_ref[...] = jnp.zeros_like(af_ref)
        mm_ref[...] = jnp.full_like(mm_ref, NEG_INF)
        lm_ref[...] = jnp.zeros_like(lm_ref)
        am_ref[...] = jnp.zeros_like(am_ref)
        car_ref[...] = jnp.zeros_like(car_ref)

    carry = car_ref[...]
    cums = []
    for i in range(pg):
        ci = _lane_cumsum(lf_refs[i][...]) + carry
        carry = jnp.broadcast_to(ci[:, LANES - 1:LANES], ci.shape)
        cums.append(ci)
    car_ref[...] = carry
    ck = jnp.concatenate(cums, axis=1)

    def fox_update(g, s, pv_fn):
        _dec_update(mf_ref, lf_ref, af_ref, g, s, pv_fn)

    def mla_update(s, pv_fn):
        _dec_update(mm_ref, lm_ref, am_ref, 0, s, pv_fn)

    kvt = jnp.concatenate([kv_refs[i][...].astype(BF16) for i in range(pg)], axis=1)
    qgs = [qf_ref[rq * g:rq * (g + 1)].reshape(rows, HEAD_DIM).astype(BF16) for g in range(FOX_KV_HEADS)]
    for g in range(FOX_KV_HEADS):
        s = _mm(qgs[g], kvt[HEAD_DIM * g:HEAD_DIM * (g + 1)])
        s = jnp.concatenate([s[r * td:(r + 1) * td] - ck[rq * g + r:rq * g + r + 1, :] for r in range(rq)], axis=0)
        vo = (FOX_KV_HEADS + g) * HEAD_DIM
        fox_update(g, s, lambda p, vo=vo: _nt(p, kvt[vo:vo + HEAD_DIM]))
    latt = jnp.concatenate([lat_refs[i][...].astype(BF16) for i in range(pg)], axis=1)
    qm = qm_ref[...].reshape(MLA_HEADS * td, MLA_LAT).astype(BF16)
    mla_update(_mm(qm, latt), lambda p: _nt(p, latt[:MLA_KV_LORA]))

    @pl.when(c == pl.num_programs(1) - 1)
    def _():
        tri = (_iota((td, td), 1) <= _iota((td, td), 0)).astype(F32)
        eye = (_iota((FOX_HEADS, FOX_HEADS), 0) == _iota((FOX_HEADS, FOX_HEADS), 1)).astype(F32)
        csn = _mm(tri, lfn_ref[...], lax.Precision.HIGHEST)
        ckn = _nt(eye, csn, lax.Precision.HIGHEST) + car_ref[:, 0:td]
        kvn = kvn_ref[...].astype(BF16)
        t_row = _iota((rows, td), 0) % td
        keep = _iota((rows, td), 1) <= t_row
        for g in range(FOX_KV_HEADS):
            s = _nt(qgs[g], kvn[:, HEAD_DIM * g:HEAD_DIM * (g + 1)])
            s = jnp.concatenate([s[r * td:(r + 1) * td] - ckn[rq * g + r:rq * g + r + 1, :] for r in range(rq)], axis=0)
            s = jnp.where(keep, s, NEG_INF)
            vo = (FOX_KV_HEADS + g) * HEAD_DIM
            fox_update(g, s, lambda p, vo=vo: _mm(p, kvn[:, vo:vo + HEAD_DIM]))
        latn = latn_ref[...].astype(BF16)
        t_row_m = _iota((MLA_HEADS * td, td), 0) % td
        keep_m = _iota((MLA_HEADS * td, td), 1) <= t_row_m
        s = jnp.where(keep_m, _nt(qm, latn), NEG_INF)
        mla_update(s, lambda p: _mm(p, latn[:, :MLA_KV_LORA]))
        for g in range(FOX_KV_HEADS):
            o = af_ref[g] / lf_ref[g]
            for r in range(rq):
                h = rq * g + r
                o_ref[:, HEAD_DIM * h:HEAD_DIM * (h + 1)] = o[r * td:(r + 1) * td]
        ol = (am_ref[0] / lm_ref[0]).astype(BF16)
        base = FOX_HEADS * HEAD_DIM
        for h in range(MLA_HEADS):
            o_ref[:, base + MLA_V * h:base + MLA_V * (h + 1)] = _mm(ol[h * td:(h + 1) * td], wuv_ref[h])


def _even_dec(page_table, qf, qm, kv32, lat32, logf, wuv, cache_kv, cache_lft, cache_lat, e, n_p, bd, td, pg):
    n_pages = page_table.shape[1]
    off = n_p // td
    rq = FOX_HEADS // FOX_KV_HEADS

    def page_spec(shape, i):
        return pl.BlockSpec((None, None) + shape, lambda b, c, pt: (e, pt[b, c * pg + i], 0, 0))

    in_specs = [
        pl.BlockSpec((FOX_HEADS, td, HEAD_DIM), lambda b, c, pt: (0, off + b, 0)),
        pl.BlockSpec((MLA_HEADS, td, MLA_LAT), lambda b, c, pt: (0, off + b, 0)),
        pl.BlockSpec((td, KV_W), lambda b, c, pt: (off + b, 0)),
        pl.BlockSpec((td, MLA_LAT), lambda b, c, pt: (off + b, 0)),
        pl.BlockSpec((td, FOX_HEADS), lambda b, c, pt: (off + b, 0)),
        pl.BlockSpec((MLA_HEADS, MLA_KV_LORA, MLA_V), lambda b, c, pt: (0, 0, 0)),
    ]
    in_specs += [page_spec((KV_W, PAGE), i) for i in range(pg)]
    in_specs += [page_spec((FOX_HEADS, PAGE), i) for i in range(pg)]
    in_specs += [page_spec((MLA_LAT, PAGE), i) for i in range(pg)]
    grid_spec = pltpu.PrefetchScalarGridSpec(
        num_scalar_prefetch=1,
        grid=(bd, n_pages // pg),
        in_specs=in_specs,
        out_specs=pl.BlockSpec((td, D_MODEL), lambda b, c, pt: (b, 0)),
        scratch_shapes=[
            pltpu.VMEM((FOX_KV_HEADS, rq * td, 1), F32), pltpu.VMEM((FOX_KV_HEADS, rq * td, 1), F32),
            pltpu.VMEM((FOX_KV_HEADS, rq * td, HEAD_DIM), F32),
            pltpu.VMEM((1, MLA_HEADS * td, 1), F32), pltpu.VMEM((1, MLA_HEADS * td, 1), F32),
            pltpu.VMEM((1, MLA_HEADS * td, MLA_KV_LORA), F32),
            pltpu.VMEM((FOX_HEADS, LANES), F32),
        ],
    )
    return pl.pallas_call(
        functools.partial(_even_dec_kernel, pg=pg, td=td),
        grid_spec=grid_spec,
        out_shape=jax.ShapeDtypeStruct((bd * td, D_MODEL), F32),
        compiler_params=_cp(("parallel", "arbitrary")),
        name="even_attn_decode",
    )(page_table, qf, qm, kv32, lat32, logf, wuv,
      *([cache_kv] * pg), *([cache_lft] * pg), *([cache_lat] * pg))


OD_COLS = 1920
N_GATES = 3 * NSA_HEADS


def _odd_proj_kernel(x_ref, g_ref, win_ref, bg_ref, rope_ref,
                     q_ref, cmp_ref, slc_ref, wn_ref, slc16_ref, wn16_ref, gate_ref):
    xn = _rms_val(x_ref[...], g_ref[...]).astype(BF16)
    z = _mm(xn, win_ref[...])
    half = ROPE_DIM // 2
    for c in range(NSA_HEADS // 2):
        qc = _rope_chunk(z[:, LANES * c:LANES * (c + 1)], rope_ref, half) * FOX_SCALE
        q_ref[2 * c] = qc[:, :HEAD_DIM]
        q_ref[2 * c + 1] = qc[:, HEAD_DIM:]
    ko, vo = NSA_HEADS * HEAD_DIM, NSA_HEADS * HEAD_DIM + 3 * LANES
    for n, (r32, r16) in enumerate(((cmp_ref, None), (slc_ref, slc16_ref), (wn_ref, wn16_ref))):
        k = _rope_chunk(z[:, ko + LANES * n:ko + LANES * (n + 1)], rope_ref, half)
        v = z[:, vo + LANES * n:vo + LANES * (n + 1)]
        r32[:, 0:LANES] = k
        r32[:, LANES:2 * LANES] = v
        if r16 is not None:
            for g in range(NSA_KV_HEADS):
                r16[g] = k[:, HEAD_DIM * g:HEAD_DIM * (g + 1)].astype(BF16)
                r16[NSA_KV_HEADS + g] = v[:, HEAD_DIM * g:HEAD_DIM * (g + 1)].astype(BF16)
    gate_ref[...] = jax.nn.sigmoid(z[:, vo + 3 * LANES:vo + 4 * LANES] + bg_ref[...])[:, 0:N_GATES]


def _odd_proj(h, g, w_in, b_gate, rope, tm, t_prompt, n_p):
    n = h.shape[0]
    npt = n_p // tm
    full = lambda shape: pl.BlockSpec(shape, lambda i: (0,) * len(shape))
    row = lambda w: pl.BlockSpec((tm, w), lambda i: (i, 0))
    hm = lambda k: pl.BlockSpec((k, tm, HEAD_DIM), lambda i: (0, i, 0))
    return pl.pallas_call(
        _odd_proj_kernel,
        grid=(n // tm,),
        in_specs=[row(D_MODEL), full((1, D_MODEL)), full((D_MODEL, OD_COLS)), full((1, LANES)),
                  _rope_spec(tm, t_prompt, npt)],
        out_specs=[hm(NSA_HEADS), row(KV_W), row(KV_W), row(KV_W), hm(4), hm(4), row(N_GATES)],
        out_shape=[
            jax.ShapeDtypeStruct((NSA_HEADS, n, HEAD_DIM), F32),
            jax.ShapeDtypeStruct((n, KV_W), F32), jax.ShapeDtypeStruct((n, KV_W), F32),
            jax.ShapeDtypeStruct((n, KV_W), F32),
            jax.ShapeDtypeStruct((4, n, HEAD_DIM), BF16), jax.ShapeDtypeStruct((4, n, HEAD_DIM), BF16),
            jax.ShapeDtypeStruct((n, N_GATES), F32),
        ],
        compiler_params=_cp(("parallel",)),
        name="odd_proj",
    )(h, g, w_in, b_gate, rope)


def _compress_rows(ch, pe_ref, wlo_ref, whi_ref, w2_ref):
    a = _mm((ch + pe_ref[0:1, :]).astype(BF16), wlo_ref[...])
    b = _mm((ch + pe_ref[1:2, :]).astype(BF16), whi_ref[...])
    hid = a + pltpu.roll(b, ch.shape[0] - 1, 0)
    return _mm(_gelu_tanh(hid).astype(BF16), w2_ref[...])


def _compress_prompt_kernel(ch_ref, pe_ref, wlo_ref, whi_ref, w2_ref, o_ref):
    o_ref[...] = _compress_rows(ch_ref[...], pe_ref, wlo_ref, whi_ref, w2_ref).astype(BF16)


def _compress_prompt(chunks, pe, wlo, whi, w2, b, rows):
    full = lambda shape: pl.BlockSpec(shape, lambda i: (0,) * len(shape))
    return pl.pallas_call(
        _compress_prompt_kernel,
        grid=(b,),
        in_specs=[pl.BlockSpec((rows, CHUNK_W), lambda i: (i, 0)),
                  full((2, CHUNK_W)), full((CHUNK_W, KV_W)), full((CHUNK_W, KV_W)), full((KV_W, KV_W))],
        out_specs=pl.BlockSpec((rows, KV_W), lambda i: (i, 0)),
        out_shape=jax.ShapeDtypeStruct((b * rows, KV_W), BF16),
        compiler_params=_cp(("parallel",)),
        name="nsa_compress_prompt",
    )(chunks, pe, wlo, whi, w2)


def _cmp_branch(q, kc, vc, qpos_col, heads, t):
    ncp = kc.shape[0]
    s = _nt(q, kc).reshape(heads, t, ncp)
    mc = (_iota((t, ncp), 1) * CMP_D + (CMP_L - 1)) <= qpos_col
    s = jnp.where(mc[None], s, NEG_INF)
    e = jnp.exp(s - jnp.max(s, axis=-1, keepdims=True))
    p = jnp.where(mc[None], e / jnp.sum(e, axis=-1, keepdims=True), 0.0)
    o = _mm(p.reshape(heads * t, ncp).astype(BF16), vc)
    return o, jnp.sum(p, axis=0)


def _select_blocks(imp_t, qpos_row, sc_ref, n_real):
    ns, c = imp_t.shape
    blk = _iota((ns, c), 0)
    cur = qpos_row // SEL_L
    forced = (blk == 0) | (blk == cur) | (blk == cur - 1)
    valid = (blk * SEL_L <= qpos_row) & (blk < n_real)
    score = jnp.where(valid, jnp.where(forced, FORCE_SCORE, imp_t), -1.0)
    score = jnp.where(blk < n_real, score, -2.0)
    sc_ref[...] = score

    def body(i, rank):
        row = sc_ref[pl.ds(i, 1), :]
        ahead = (row > score) | ((row == score) & (i < blk))
        return rank + ahead.astype(F32)

    rank = lax.fori_loop(0, n_real, body, jnp.zeros((ns, c), F32))
    return (rank < float(min(TOP_N, n_real))).astype(F32)


def _imp_matrix(ns_pad, ncp):
    ratio, span = SEL_L // CMP_D, CMP_L // CMP_D
    m = np.zeros((ns_pad, ncp), np.float32)
    for j in range(ns_pad):
        for a in range(ratio):
            for b in range(span):
                cc = j * ratio + a + b
                if cc < ncp:
                    m[j, cc] += 1.0
    return jnp.asarray(m)


def _expand_matrix(ns_pad, n_keys):
    m = (np.arange(n_keys)[None, :] // SEL_L == np.arange(ns_pad)[:, None]).astype(np.float32)
    return jnp.asarray(m, dtype=BF16)


def _nsa_prompt_kernel(q_ref, gate_ref, kvc_ref, slc_ref, wn_ref, mimp_ref, eye_ref, exp_ref, o_ref,
                       m_ref, l_ref, acc_ref, mask_ref, sc_ref, *, tq, tk, t):
    qi = pl.program_id(1)
    qs = qi * tq
    rq = NSA_HEADS // NSA_KV_HEADS
    ns = t // SEL_L
    qpos_col = qs + _iota((tq, 1), 0)
    qpos_row = qs + _iota((1, tq), 1)
    kvc = kvc_ref[...]
    gates = gate_ref[...]
    nch = tk // LANES
    causal = (_iota((tq, t), 1) <= qs + _iota((tq, t), 0)).astype(F32)
    wspan = min(WINDOW + tq, t)
    w0 = pl.multiple_of(jnp.minimum(jnp.maximum(qs - WINDOW, 0), t - wspan), tq)
    dwin = (qs + _iota((tq, wspan), 0)) - (w0 + _iota((tq, wspan), 1))
    keep_win = (dwin >= 0) & (dwin < WINDOW)

    def masked_chunks(s, keep):
        out = []
        for c in range(s.shape[1] // LANES):
            sl = slice(LANES * c, LANES * (c + 1))
            out.append(jnp.where(keep[:, sl][None], s[:, sl].reshape(rq, tq, LANES), NEG_INF)
                       .reshape(rq * tq, LANES))
        return out

    for g in range(NSA_KV_HEADS):
        q = q_ref[rq * g:rq * (g + 1)].reshape(rq * tq, HEAD_DIM).astype(BF16)
        kc = kvc[:, HEAD_DIM * g:HEAD_DIM * (g + 1)]
        vc = kvc[:, HEAD_DIM * (NSA_KV_HEADS + g):HEAD_DIM * (NSA_KV_HEADS + g + 1)]
        o_cmp, pgrp = _cmp_branch(q, kc, vc, qpos_col, rq, tq)
        imp_t = _nt(mimp_ref[...], pgrp, lax.Precision.HIGHEST)
        sel_t = _select_blocks(imp_t, qpos_row, sc_ref, ns)
        sel = _nt(eye_ref[...], sel_t.astype(BF16))
        mfull = _mm(sel.astype(BF16), exp_ref[...]) * causal
        for jj in range(t // tk):
            mask_ref[jj] = mfull[:, jj * tk:(jj + 1) * tk]

        m_ref[...] = jnp.full_like(m_ref, NEG_INF)
        l_ref[...] = jnp.zeros_like(l_ref)
        acc_ref[...] = jnp.zeros_like(acc_ref)

        def slc_body(j, c):
            ks = pl.multiple_of(j * tk, tk)
            k = slc_ref[g, pl.ds(ks, tk), :]
            v = slc_ref[NSA_KV_HEADS + g, pl.ds(ks, tk), :]
            _online_update(masked_chunks(_nt(q, k), mask_ref[j] > 0.5), m_ref, l_ref, acc_ref, v, 0)
            return c

        n_all = (qs + tq - 1) // tk + 1
        lax.fori_loop(0, n_all, slc_body, 0)
        o_slc = _online_finish(l_ref, acc_ref, 0)

        kw = wn_ref[g, pl.ds(w0, wspan), :]
        vw = wn_ref[NSA_KV_HEADS + g, pl.ds(w0, wspan), :]
        chunks = masked_chunks(_nt(q, kw), keep_win)
        mx = chunks[0]
        for c in chunks[1:]:
            mx = jnp.maximum(mx, c)
        mx = jnp.broadcast_to(jnp.max(mx, axis=-1, keepdims=True), mx.shape)
        ps = [jnp.exp(c - mx) for c in chunks]
        den = ps[0]
        for p in ps[1:]:
            den = den + p
        o_win = _mm(jnp.concatenate(ps, axis=1).astype(BF16), vw) / jnp.sum(den, axis=-1, keepdims=True)

        for r in range(rq):
            h = rq * g + r
            rows = slice(r * tq, (r + 1) * tq)
            o = (gates[:, h:h + 1] * o_cmp[rows] + gates[:, NSA_HEADS + h:NSA_HEADS + h + 1] * o_slc[rows]
                 + gates[:, 2 * NSA_HEADS + h:2 * NSA_HEADS + h + 1] * o_win[rows])
            o_ref[:, HEAD_DIM * h:HEAD_DIM * (h + 1)] = o.astype(BF16)


def _nsa_prompt(q, gates, kvc, slc16, wn16, mimp, eye, expand, b, t, tq, tk):
    nq = t // tq
    rq = NSA_HEADS // NSA_KV_HEADS
    ncp = t // CMP_D
    ns = t // SEL_L
    return pl.pallas_call(
        functools.partial(_nsa_prompt_kernel, tq=tq, tk=tk, t=t),
        grid=(b, nq),
        in_specs=[
            pl.BlockSpec((NSA_HEADS, tq, HEAD_DIM), lambda i, j: (0, i * nq + j, 0)),
            pl.BlockSpec((tq, N_GATES), lambda i, j: (i * nq + j, 0)),
            pl.BlockSpec((ncp, KV_W), lambda i, j: (i, 0)),
            pl.BlockSpec((4, t, HEAD_DIM), lambda i, j: (0, i, 0)),
            pl.BlockSpec((4, t, HEAD_DIM), lambda i, j: (0, i, 0)),
            pl.BlockSpec((ns, ncp), lambda i, j: (0, 0)),
            pl.BlockSpec((tq, tq), lambda i, j: (0, 0)),
            pl.BlockSpec((ns, t), lambda i, j: (0, 0)),
        ],
        out_specs=pl.BlockSpec((tq, D_MODEL), lambda i, j: (i * nq + j, 0)),
        out_shape=jax.ShapeDtypeStruct((b * t, D_MODEL), BF16),
        scratch_shapes=[
            pltpu.VMEM((1, rq * tq, LANES), F32), pltpu.VMEM((1, rq * tq, LANES), F32),
            pltpu.VMEM((1, rq * tq, HEAD_DIM), F32),
            pltpu.VMEM((t // tk, tq, tk), F32),
            pltpu.VMEM((ns, tq), F32),
        ],
        compiler_params=_cp(("parallel", "arbitrary")),
        name="nsa_attn_prompt",
    )(q, gates, kvc, slc16, wn16, mimp, eye, expand)


def _nsa_dec1_kernel(pt_ref, q_ref, pe_ref, wlo_ref, whi_ref, w2_ref, mimp_ref, exp_ref, *rest,
                     n_pages, td, past_len, ns_pad):
    pages = rest[:n_pages]
    ocmp_ref, mask_ref, sc_ref, xs_ref = rest[n_pages:]
    rq = NSA_HEADS // NSA_KV_HEADS
    n_real = -(-(past_len + td) // SEL_L)
    nchunk = past_len // CMP_D
    for i in range(n_pages):
        x = pages[i][...].T
        xs_ref[0, PAGE * i:PAGE * (i + 1), :] = x[:, :LANES]
        xs_ref[1, PAGE * i:PAGE * (i + 1), :] = x[:, LANES:]
    a = jnp.zeros((nchunk, KV_W), F32)
    b = jnp.zeros((nchunk, KV_W), F32)
    for l in range(CMP_D):
        r = jnp.concatenate([xs_ref[0, pl.ds(l, nchunk, stride=CMP_D), :],
                             xs_ref[1, pl.ds(l, nchunk, stride=CMP_D), :]], axis=1)
        ws = slice(KV_W * l, KV_W * (l + 1))
        a = a + _mm((r + pe_ref[0:1, ws]).astype(BF16), wlo_ref[ws, :])
        b = b + _mm((r + pe_ref[1:2, ws]).astype(BF16), whi_ref[ws, :])
    hid = a + pltpu.roll(b, nchunk - 1, 0)
    kvc = _mm(_gelu_tanh(hid).astype(BF16), w2_ref[...]).astype(BF16)
    qpos_col = past_len + _iota((td, 1), 0)
    pgs = []
    for g in range(NSA_KV_HEADS):
        q = q_ref[rq * g:rq * (g + 1)].reshape(rq * td, HEAD_DIM).astype(BF16)
        kc = kvc[:, HEAD_DIM * g:HEAD_DIM * (g + 1)]
        vc = kvc[:, HEAD_DIM * (NSA_KV_HEADS + g):HEAD_DIM * (NSA_KV_HEADS + g + 1)]
        o_cmp, pgrp = _cmp_branch(q, kc, vc, qpos_col, rq, td)
        for r in range(rq):
            h = rq * g + r
            ocmp_ref[:, HEAD_DIM * h:HEAD_DIM * (h + 1)] = o_cmp[r * td:(r + 1) * td]
        pgs.append(pgrp)
    pg_all = jnp.concatenate(pgs, axis=0)
    imp_t = _nt(mimp_ref[...], pg_all, lax.Precision.HIGHEST)
    qpos_row = past_len + (_iota((1, NSA_KV_HEADS * td), 1) % td)
    sel_t = _select_blocks(imp_t, qpos_row, sc_ref, n_real)
    c = NSA_KV_HEADS * td
    eye = (_iota((c, c), 0) == _iota((c, c), 1)).astype(BF16)
    sel = _nt(eye, sel_t.astype(BF16))
    mask_ref[...] = _mm(sel.astype(BF16), exp_ref[...]).astype(BF16)


def _nsa_dec1(page_table, q, pe, wlo, whi, w2, mimp, expand, cache_chunks, e, n_p, bd, td, ns_pad):
    n_pages = page_table.shape[1]
    past_len = n_pages * PAGE
    off = n_p // td
    ncp = past_len // CMP_D
    full = lambda shape: pl.BlockSpec(shape, lambda b, pt: (0,) * len(shape))
    in_specs = [
        pl.BlockSpec((NSA_HEADS, td, HEAD_DIM), lambda b, pt: (0, off + b, 0)),
        full((2, CHUNK_W)), full((CHUNK_W, KV_W)), full((CHUNK_W, KV_W)), full((KV_W, KV_W)),
        full((ns_pad, ncp)), full((ns_pad, past_len)),
    ]
    in_specs += [pl.BlockSpec((None, None, KV_W, PAGE), functools.partial(
        lambda b, pt, i: (e, pt[b, i], 0, 0), i=i)) for i in range(n_pages)]
    grid_spec = pltpu.PrefetchScalarGridSpec(
        num_scalar_prefetch=1,
        grid=(bd,),
        in_specs=in_specs,
        out_specs=[pl.BlockSpec((td, D_MODEL), lambda b, pt: (b, 0)),
                   pl.BlockSpec((None, NSA_KV_HEADS * td, past_len), lambda b, pt: (b, 0, 0))],
        scratch_shapes=[pltpu.VMEM((ns_pad, NSA_KV_HEADS * td), F32),
                        pltpu.VMEM((2, past_len, LANES), F32)],
    )
    return pl.pallas_call(
        functools.partial(_nsa_dec1_kernel, n_pages=n_pages, td=td, past_len=past_len, ns_pad=ns_pad),
        grid_spec=grid_spec,
        out_shape=[jax.ShapeDtypeStruct((bd * td, D_MODEL), F32),
                   jax.ShapeDtypeStruct((bd, NSA_KV_HEADS * td, past_len), BF16)],
        compiler_params=_cp(("parallel",)),
        name="nsa_decode_select",
    )(page_table, q, pe, wlo, whi, w2, mimp, expand, *([cache_chunks] * n_pages))


def _nsa_dec2_kernel(pt_ref, q_ref, gate_ref, ocmp_ref, mask_ref, slcn_ref, wnn_ref, state_ref, *rest,
                     pg, td, past_len):
    pages = rest[:pg]
    o_ref, nst_ref, m_ref, l_ref, acc_ref = rest[pg:]
    c = pl.program_id(1)
    rq = NSA_HEADS // NSA_KV_HEADS
    rows = rq * td
    wbuf = state_ref.shape[1]

    @pl.when(c == 0)
    def _():
        m_ref[...] = jnp.full_like(m_ref, NEG_INF)
        l_ref[...] = jnp.zeros_like(l_ref)
        acc_ref[...] = jnp.zeros_like(acc_ref)

    kvt = jnp.concatenate([pages[i][...].astype(BF16) for i in range(pg)], axis=1)
    qgs = [q_ref[rq * g:rq * (g + 1)].reshape(rows, HEAD_DIM).astype(BF16) for g in range(NSA_KV_HEADS)]
    msk = mask_ref[...].astype(F32)
    for g in range(NSA_KV_HEADS):
        s = _mm(qgs[g], kvt[HEAD_DIM * g:HEAD_DIM * (g + 1)])
        keep = jnp.broadcast_to((msk[td * g:td * (g + 1)] > 0.5)[None], (rq, td, pg * PAGE)).reshape(rows, pg * PAGE)
        s = jnp.where(keep, s, NEG_INF)
        vo = (NSA_KV_HEADS + g) * HEAD_DIM
        _dec_update(m_ref, l_ref, acc_ref, g, s, lambda p, vo=vo: _nt(p, kvt[vo:vo + HEAD_DIM]))

    @pl.when(c == pl.num_programs(1) - 1)
    def _():
        gates = gate_ref[...]
        ocmp = ocmp_ref[...]
        slcn = slcn_ref[...].astype(BF16)
        wnn = wnn_ref[...]
        st = state_ref[...]
        eye = (_iota((KV_W, KV_W), 0) == _iota((KV_W, KV_W), 1)).astype(F32)
        place = (_iota((td, wbuf), 1) == _iota((td, wbuf), 0) + (wbuf - td)).astype(F32)
        wnn_t = _nt(eye, wnn, lax.Precision.HIGHEST)
        tail = _mm(wnn_t, place, lax.Precision.HIGHEST)
        nst_ref[...] = jnp.where(_iota((KV_W, wbuf), 1) < wbuf - td, pltpu.roll(st, wbuf - td, 1), tail)
        st16 = st.astype(BF16)
        wnn16 = wnn.astype(BF16)
        t_row = _iota((rows, td), 0) % td
        keep_new = _iota((rows, td), 1) <= t_row
        t_row_w = _iota((rows, wbuf), 0) % td
        dpos = (past_len + t_row_w) - (past_len - wbuf + _iota((rows, wbuf), 1))
        keep_st = (dpos >= 0) & (dpos < WINDOW)
        for g in range(NSA_KV_HEADS):
            ko, vo = HEAD_DIM * g, HEAD_DIM * (NSA_KV_HEADS + g)
            s = jnp.where(keep_new, _nt(qgs[g], slcn[:, ko:ko + HEAD_DIM]), NEG_INF)
            _dec_update(m_ref, l_ref, acc_ref, g, s, lambda p, vo=vo: _mm(p, slcn[:, vo:vo + HEAD_DIM]))
            o_slc = acc_ref[g] / l_ref[g]
            s1 = jnp.where(keep_st, _mm(qgs[g], st16[ko:ko + HEAD_DIM]), NEG_INF)
            s2 = jnp.where(keep_new, _nt(qgs[g], wnn16[:, ko:ko + HEAD_DIM]), NEG_INF)
            mx = jnp.maximum(jnp.max(s1, axis=-1, keepdims=True), jnp.max(s2, axis=-1, keepdims=True))
            p1 = jnp.exp(s1 - mx)
            p2 = jnp.exp(s2 - mx)
            den = jnp.sum(p1, axis=-1, keepdims=True) + jnp.sum(p2, axis=-1, keepdims=True)
            o_win = (_nt(p1.astype(BF16), st16[vo:vo + HEAD_DIM])
                     + _mm(p2.astype(BF16), wnn16[:, vo:vo + HEAD_DIM])) / den
            for r in range(rq):
                h = rq * g + r
                rs = slice(r * td, (r + 1) * td)
                cols = slice(HEAD_DIM * h, HEAD_DIM * (h + 1))
                o_ref[:, cols] = (gates[:, h:h + 1] * ocmp[:, cols]
                                  + gates[:, NSA_HEADS + h:NSA_HEADS + h + 1] * o_slc[rs]
                                  + gates[:, 2 * NSA_HEADS + h:2 * NSA_HEADS + h + 1] * o_win[rs])


def _nsa_dec2(page_table, q, gates, ocmp, mask, slc32, wn32, state, cache_slc, e, n_p, bd, td, pg):
    n_pages = page_table.shape[1]
    past_len = n_pages * PAGE
    off = n_p // td
    rq = NSA_HEADS // NSA_KV_HEADS
    wbuf = state.shape[3]
    in_specs = [
        pl.BlockSpec((NSA_HEADS, td, HEAD_DIM), lambda b, c, pt: (0, off + b, 0)),
        pl.BlockSpec((td, N_GATES), lambda b, c, pt: (off + b, 0)),
        pl.BlockSpec((td, D_MODEL), lambda b, c, pt: (b, 0)),
        pl.BlockSpec((None, NSA_KV_HEADS * td, pg * PAGE), lambda b, c, pt: (b, 0, c)),
        pl.BlockSpec((td, KV_W), lambda b, c, pt: (off + b, 0)),
        pl.BlockSpec((td, KV_W), lambda b, c, pt: (off + b, 0)),
        pl.BlockSpec((None, None, KV_W, wbuf), lambda b, c, pt: (e, b, 0, 0)),
    ]
    in_specs += [pl.BlockSpec((None, None, KV_W, PAGE), functools.partial(
        lambda b, c, pt, i: (e, pt[b, c * pg + i], 0, 0), i=i)) for i in range(pg)]
    grid_spec = pltpu.PrefetchScalarGridSpec(
        num_scalar_prefetch=1,
        grid=(bd, n_pages // pg),
        in_specs=in_specs,
        out_specs=[pl.BlockSpec((td, D_MODEL), lambda b, c, pt: (b, 0)),
                   pl.BlockSpec((None, KV_W, wbuf), lambda b, c, pt: (b, 0, 0))],
        scratch_shapes=[
            pltpu.VMEM((NSA_KV_HEADS, rq * td, 1), F32), pltpu.VMEM((NSA_KV_HEADS, rq * td, 1), F32),
            pltpu.VMEM((NSA_KV_HEADS, rq * td, HEAD_DIM), F32),
        ],
    )
    return pl.pallas_call(
        functools.partial(_nsa_dec2_kernel, pg=pg, td=td, past_len=past_len),
        grid_spec=grid_spec,
        out_shape=[jax.ShapeDtypeStruct((bd * td, D_MODEL), F32),
                   jax.ShapeDtypeStruct((bd, KV_W, wbuf), F32)],
        compiler_params=_cp(("parallel", "arbitrary")),
        name="nsa_decode_attend",
    )(page_table, q, gates, ocmp, mask, slc32, wn32, state, *([cache_slc] * pg))


def _prep_even(w_in, b_f, w_q_up, w_uk, w_uv):
    pad = jnp.zeros((D_MODEL, EV_COLS - 1192), F32)
    w = jnp.concatenate([w_in[:, 0:768], w_in[:, 776:1192], w_in[:, 768:776], pad], axis=1).astype(BF16)
    bf = jnp.zeros((1, LANES), F32).at[0, EV_FLG_LANE:EV_FLG_LANE + FOX_HEADS].set(b_f)
    wq = w_q_up.reshape(MLA_Q_LORA, MLA_HEADS, MLA_QK)
    wq = jnp.concatenate([wq[:, :, :MLA_NOPE].reshape(MLA_Q_LORA, -1),
                          wq[:, :, MLA_NOPE:].reshape(MLA_Q_LORA, -1)], axis=1).astype(BF16)
    wuk_h = jnp.transpose(w_uk, (1, 2, 0))
    eye = jnp.eye(MLA_HEADS, dtype=F32)
    wuk_bd = (eye[:, None, :, None] * wuk_h[:, :, None, :]).reshape(MLA_HEADS * MLA_NOPE, MLA_HEADS * MLA_KV_LORA)
    wuv = jnp.transpose(w_uv, (1, 0, 2)).astype(BF16)
    return w, bf, wq, wuk_bd.astype(BF16), wuv


def _prep_odd(w_in, b_gate, pe, w1, w2):
    qw = NSA_HEADS * HEAD_DIM
    seg = lambda n: w_in[:, qw + LANES * n:qw + LANES * (n + 1)]
    gcol = w_in[:, qw + 6 * LANES:qw + 6 * LANES + N_GATES]
    perm = np.array([h * 3 + c for c in range(3) for h in range(NSA_HEADS)])
    pad = jnp.zeros((D_MODEL, LANES - N_GATES), F32)
    w = jnp.concatenate([w_in[:, :qw], seg(0), seg(2), seg(4), seg(1), seg(3), seg(5), gcol[:, perm], pad],
                        axis=1).astype(BF16)
    bg = jnp.zeros((1, LANES), F32).at[0, 0:N_GATES].set(b_gate[perm])
    kv_of = (0, 0, 1, 1)
    pe_rows, w_halves = [], []
    eye4 = jnp.eye(4, dtype=F32)
    for half in range(2):
        ls = slice(half * CMP_D, (half + 1) * CMP_D)
        pe_rows.append(jnp.stack([pe[kv_of[j], ls, :] for j in range(4)], axis=1).reshape(CHUNK_W))
        w1r = jnp.stack([w1[kv_of[j]].reshape(CMP_L, HEAD_DIM, HEAD_DIM)[ls] for j in range(4)], axis=1)
        wexp = (w1r[:, :, :, None, :] * eye4[None, :, None, :, None]).reshape(CHUNK_W, KV_W)
        w_halves.append(wexp.astype(BF16))
    w2bd = (eye4[:, None, :, None] * jnp.stack([w2[kv_of[j]] for j in range(4)])[:, :, None, :]).reshape(KV_W, KV_W)
    return w, bg, jnp.stack(pe_rows), w_halves[0], w_halves[1], w2bd.astype(BF16)


def kernel(x_prompt, x_sample, cache_fox_kv, cache_fox_logf, cache_mla, cache_nsa_cmp, cache_nsa_slc,
           state_nsa_win, page_table, norm_w, final_norm, ffn_w_gate, ffn_w_up, ffn_w_down,
           ev_w_in, ev_b_f, mla_q_norm, mla_w_q_up, mla_kv_norm, mla_w_uk, mla_w_uv, ev_w_out,
           od_w_in, od_b_gate, nsa_cmp_pe, nsa_cmp_w1, nsa_cmp_w2, od_w_out):
    b, t, _ = x_prompt.shape
    bd, td, _ = x_sample.shape
    depth = norm_w.shape[0]
    n_pages = page_table.shape[1]
    past_len = n_pages * PAGE
    n_p, n_s = b * t, bd * td
    n = n_p + n_s
    wbuf = state_nsa_win.shape[2]

    tm = math.gcd(512, n_s)
    tf = FF_DIM // 2
    tq = min(256, t)
    tk = min(512, t)
    pg = min(16, n_pages)
    ns_pad = -(-(-(-(past_len + td) // SEL_L)) // 16) * 16

    rope_mla = _rope_tables(t, past_len, td, tm, MLA_THETA, MLA_ROPE, MLA_ROPE)
    rope_nsa = _rope_tables(t, past_len, td, tm, ROPE_THETA, ROPE_DIM, HEAD_DIM)
    mimp_p = _imp_matrix(t // SEL_L, t // CMP_D)
    mimp_s = _imp_matrix(ns_pad, past_len // CMP_D)
    expand_p = _expand_matrix(t // SEL_L, t)
    expand_s = _expand_matrix(ns_pad, past_len)
    eye_q = jnp.eye(tq, dtype=BF16)

    wg16, wu16, wd16 = ffn_w_gate.astype(BF16), ffn_w_up.astype(BF16), ffn_w_down.astype(BF16)
    tok_minor = lambda c: jnp.transpose(c, (0, 1, 3, 4, 5, 2)).reshape(c.shape[0], c.shape[1], KV_W, c.shape[2])
    cache_kv2 = tok_minor(cache_fox_kv)
    cache_lft = jnp.swapaxes(cache_fox_logf, 2, 3)
    cache_lat2 = jnp.swapaxes(cache_mla, 2, 3)
    cache_cmp2 = tok_minor(cache_nsa_cmp)
    cache_slc2 = tok_minor(cache_nsa_slc)
    state2 = tok_minor(state_nsa_win)

    h = jnp.concatenate([x_prompt.reshape(n_p, D_MODEL), x_sample.reshape(n_s, D_MODEL)], axis=0)
    row = lambda v: v.reshape(1, -1)
    kv_shape = (2, FOX_KV_HEADS, HEAD_DIM)
    outs = {k: [] for k in ("fkv_p", "fkv_s", "flf_p", "flf_s", "mla_p", "mla_s",
                            "cmp_p", "cmp_s", "slc_p", "slc_s", "win_p", "win_s")}
    for li in range(depth):
        e = li // 2
        h = _ffn(h, row(norm_w[li, 0]), wg16[li, 0], wu16[li, 0], wd16[li, 0], tm, tf)
        if li % 2 == 0:
            w, bf, wq, wuk_bd, wuv = _prep_even(ev_w_in[e], ev_b_f[e], mla_w_q_up[e], mla_w_uk[e], mla_w_uv[e])
            qf, kv32, kv16, logf, qm, lat32, lat16 = _even_proj(
                h, row(norm_w[li, 1]), w, bf, row(mla_q_norm[e]), wq, wuk_bd, row(mla_kv_norm[e]),
                rope_mla, tm, t, n_p)
            csum, cst = _csum(logf, b, t, tk)
            o_p = _even_attn(qf, qm, csum, kv16, lat16, cst, wuv, b, t, tq, tk)
            o_s = _even_dec(page_table, qf, qm, kv32, lat32, logf, wuv, cache_kv2, cache_lft, cache_lat2,
                            e, n_p, bd, td, pg)
            h = _out_proj(h, o_p, o_s, ev_w_out[e].astype(BF16), tm)
            outs["fkv_p"].append(kv32[:n_p].reshape((b, t) + kv_shape))
            outs["fkv_s"].append(kv32[n_p:].reshape((bd, td) + kv_shape))
            outs["flf_p"].append(logf[:n_p].reshape(b, t, FOX_HEADS))
            outs["flf_s"].append(logf[n_p:].reshape(bd, td, FOX_HEADS))
            outs["mla_p"].append(lat32[:n_p].reshape(b, t, MLA_LAT))
            outs["mla_s"].append(lat32[n_p:].reshape(bd, td, MLA_LAT))
        else:
            w, bg, pe2, wlo, whi, w2bd = _prep_odd(od_w_in[e], od_b_gate[e], nsa_cmp_pe[e], nsa_cmp_w1[e],
                                                   nsa_cmp_w2[e])
            q, cmp32, slc32, wn32, slc16, wn16, gates = _odd_proj(
                h, row(norm_w[li, 1]), w, bg, rope_nsa, tm, t, n_p)
            kvc = _compress_prompt(cmp32[:n_p].reshape(n_p // CMP_D, CHUNK_W), pe2, wlo, whi, w2bd,
                                   b, t // CMP_D)
            o_p = _nsa_prompt(q, gates, kvc, slc16, wn16, mimp_p, eye_q, expand_p, b, t, tq, tk)
            ocmp, mask = _nsa_dec1(page_table, q, pe2, wlo, whi, w2bd, mimp_s, expand_s, cache_cmp2,
                                   e, n_p, bd, td, ns_pad)
            o_s, nst = _nsa_dec2(page_table, q, gates, ocmp, mask, slc32, wn32, state2, cache_slc2,
                                 e, n_p, bd, td, pg)
            h = _out_proj(h, o_p, o_s, od_w_out[e].astype(BF16), tm)
            wlen = min(WINDOW, t)
            outs["cmp_p"].append(cmp32[:n_p].reshape((b, t) + kv_shape))
            outs["cmp_s"].append(cmp32[n_p:].reshape((bd, td) + kv_shape))
            outs["slc_p"].append(slc32[:n_p].reshape((b, t) + kv_shape))
            outs["slc_s"].append(slc32[n_p:].reshape((bd, td) + kv_shape))
            outs["win_p"].append(wn32[:n_p].reshape((b, t) + kv_shape)[:, t - wlen:])
            outs["win_s"].append(jnp.transpose(nst.reshape((bd,) + kv_shape + (wbuf,)), (0, 4, 1, 2, 3)))
        h = _ffn(h, row(norm_w[li, 2]), wg16[li, 1], wu16[li, 1], wd16[li, 1], tm, tf)
    y = _final_norm(h, row(final_norm), tm)
    st = lambda k: jnp.stack(outs[k])
    return (y[:n_p].reshape(b, t, D_MODEL), y[n_p:].reshape(bd, td, D_MODEL),
            st("fkv_p"), st("fkv_s"), st("flf_p"), st("flf_s"), st("mla_p"), st("mla_s"),
            st("cmp_p"), st("cmp_s"), st("slc_p"), st("slc_s"), st("win_p"), st("win_s"))
```

```python
import functools
import math

import numpy as np
import jax
import jax.numpy as jnp
from jax import lax
from jax.experimental import pallas as pl
from jax.experimental.pallas import tpu as pltpu

F32 = jnp.float32
BF16 = jnp.bfloat16

D_MODEL = 1024
HEAD_DIM = 64
FOX_HEADS = 8
FOX_KV_HEADS = 2
MLA_HEADS = 8
MLA_Q_LORA = 256
MLA_KV_LORA = 128
MLA_NOPE = 64
MLA_ROPE = 32
MLA_V = 64
MLA_THETA = 10000.0
NSA_HEADS = 16
NSA_KV_HEADS = 2
CMP_L = 32
CMP_D = 16
SEL_L = 64
TOP_N = 16
WINDOW = 512
ROPE_THETA = 500000.0
ROPE_DIM = HEAD_DIM // 4
FF_DIM = 2816
EPS = 1e-6
FORCE_SCORE = 1e9
NEG_INF = -1e30
PAGE = 128

LANES = 128
MLA_LAT = MLA_KV_LORA + MLA_ROPE
MLA_QK = MLA_NOPE + MLA_ROPE
FOX_SCALE = HEAD_DIM ** -0.5
MLA_SCALE = MLA_QK ** -0.5
KV_W = 2 * FOX_KV_HEADS * HEAD_DIM
CHUNK_W = CMP_D * KV_W
VMEM_LIMIT = 56 * 1024 * 1024


def _cp(sem):
    return pltpu.CompilerParams(dimension_semantics=sem, vmem_limit_bytes=VMEM_LIMIT)


def _nt(a, b, precision=None):
    return lax.dot_general(a, b, (((1,), (1,)), ((), ())), precision=precision,
                           preferred_element_type=F32)


def _mm(a, b, precision=None):
    return jnp.dot(a, b, precision=precision, preferred_element_type=F32)


def _rms_val(x, g):
    return (x * lax.rsqrt(jnp.mean(x * x, axis=-1, keepdims=True) + EPS)) * g


def _log_sigmoid(x):
    return -(jnp.maximum(-x, 0.0) + jnp.log1p(jnp.exp(-jnp.abs(x))))


def _gelu_tanh(x):
    return x * (0.5 * (1.0 + jnp.tanh(math.sqrt(2.0 / math.pi) * (x + 0.044715 * (x * x * x)))))


def _iota(shape, dim):
    return lax.broadcasted_iota(jnp.int32, shape, dim)


def _ffn_kernel(x_ref, g_ref, wg_ref, wu_ref, wd_ref, o_ref, xn_ref, acc_ref):
    j = pl.program_id(1)

    @pl.when(j == 0)
    def _():
        xn_ref[...] = _rms_val(x_ref[...], g_ref[...]).astype(BF16)
        acc_ref[...] = jnp.zeros_like(acc_ref)

    xn = xn_ref[...]
    g = _mm(xn, wg_ref[...])
    u = _mm(xn, wu_ref[...])
    a = (g * jax.nn.sigmoid(g)) * u
    acc_ref[...] += _mm(a.astype(BF16), wd_ref[...])

    @pl.when(j == pl.num_programs(1) - 1)
    def _():
        o_ref[...] = x_ref[...] + 0.5 * acc_ref[...]


def _ffn(h, g, wg, wu, wd, tm, tf):
    n = h.shape[0]
    return pl.pallas_call(
        _ffn_kernel,
        grid=(n // tm, FF_DIM // tf),
        in_specs=[
            pl.BlockSpec((tm, D_MODEL), lambda i, j: (i, 0)),
            pl.BlockSpec((1, D_MODEL), lambda i, j: (0, 0)),
            pl.BlockSpec((D_MODEL, tf), lambda i, j: (0, j)),
            pl.BlockSpec((D_MODEL, tf), lambda i, j: (0, j)),
            pl.BlockSpec((tf, D_MODEL), lambda i, j: (j, 0)),
        ],
        out_specs=pl.BlockSpec((tm, D_MODEL), lambda i, j: (i, 0)),
        out_shape=jax.ShapeDtypeStruct((n, D_MODEL), F32),
        scratch_shapes=[pltpu.VMEM((tm, D_MODEL), BF16), pltpu.VMEM((tm, D_MODEL), F32)],
        compiler_params=_cp(("parallel", "arbitrary")),
        name="ffn",
    )(h, g, wg, wu, wd)


def _out_proj_kernel(h_ref, op_ref, os_ref, w_ref, o_ref, *, npt):
    i = pl.program_id(0)

    @pl.when(i < npt)
    def _():
        o_ref[...] = h_ref[...] + _mm(op_ref[...], w_ref[...])

    @pl.when(i >= npt)
    def _():
        o_ref[...] = h_ref[...] + _mm(os_ref[...].astype(BF16), w_ref[...])


def _out_proj(h, o_p, o_s, w, tm):
    n = h.shape[0]
    npt = o_p.shape[0] // tm
    return pl.pallas_call(
        functools.partial(_out_proj_kernel, npt=npt),
        grid=(n // tm,),
        in_specs=[
            pl.BlockSpec((tm, D_MODEL), lambda i: (i, 0)),
            pl.BlockSpec((tm, D_MODEL), lambda i: (jnp.minimum(i, npt - 1), 0)),
            pl.BlockSpec((tm, D_MODEL), lambda i: (jnp.maximum(i - npt, 0), 0)),
            pl.BlockSpec((D_MODEL, D_MODEL), lambda i: (0, 0)),
        ],
        out_specs=pl.BlockSpec((tm, D_MODEL), lambda i: (i, 0)),
        out_shape=jax.ShapeDtypeStruct((n, D_MODEL), F32),
        compiler_params=_cp(("parallel",)),
        name="out_proj",
    )(h, o_p, o_s, w)


def _final_norm_kernel(x_ref, g_ref, o_ref):
    o_ref[...] = _rms_val(x_ref[...], g_ref[...])


def _final_norm(h, g, tm):
    n = h.shape[0]
    return pl.pallas_call(
        _final_norm_kernel,
        grid=(n // tm,),
        in_specs=[pl.BlockSpec((tm, D_MODEL), lambda i: (i, 0)),
                  pl.BlockSpec((1, D_MODEL), lambda i: (0, 0))],
        out_specs=pl.BlockSpec((tm, D_MODEL), lambda i: (i, 0)),
        out_shape=jax.ShapeDtypeStruct((n, D_MODEL), F32),
        compiler_params=_cp(("parallel",)),
        name="final_norm",
    )(h, g)


def _rope_tables(t_prompt, past_len, t_dec, tm, theta, rot_dim, period):
    half = rot_dim // 2
    pos = np.concatenate([np.arange(t_prompt), past_len + (np.arange(tm) % t_dec)]).astype(np.float64)
    inv = np.float64(theta) ** (-np.arange(half, dtype=np.float64) / half)
    ang = pos[:, None] * inv[None, :]
    cos, sin = np.cos(ang), np.sin(ang)
    lane = np.arange(LANES) % period
    c = np.ones((pos.shape[0], LANES))
    s1 = np.zeros((pos.shape[0], LANES))
    s2 = np.zeros((pos.shape[0], LANES))
    lo = lane < half
    hi = (lane >= half) & (lane < rot_dim)
    c[:, lo] = cos[:, lane[lo]]
    c[:, hi] = cos[:, lane[hi] - half]
    s1[:, hi] = sin[:, lane[hi] - half]
    s2[:, lo] = -sin[:, lane[lo]]
    return jnp.asarray(np.stack([c, s1, s2]), dtype=F32)


def _rope_chunk(x, rope_ref, half):
    return (x * rope_ref[0] + pltpu.roll(x, half, 1) * rope_ref[1]
            + pltpu.roll(x, LANES - half, 1) * rope_ref[2])


def _rope_spec(tm, t_prompt, npt):
    tiles = t_prompt // tm
    return pl.BlockSpec((3, tm, LANES), lambda i: (0, jnp.where(i < npt, i % tiles, tiles), 0))


EV_COLS = 1280
EV_FLG_LANE = 32


def _even_proj_kernel(x_ref, g_ref, win_ref, bf_ref, qn_ref, wqu_ref, wuk_ref, kvn_ref, rope_ref,
                      qf_ref, kv32_ref, kv16_ref, logf_ref, qm_ref, lat32_ref, lat16_ref):
    xn = _rms_val(x_ref[...], g_ref[...]).astype(BF16)
    z = _mm(xn, win_ref[...])
    for h in range(FOX_HEADS):
        qf_ref[h] = z[:, HEAD_DIM * h:HEAD_DIM * (h + 1)] * FOX_SCALE
    kv = z[:, 512:768]
    kv32_ref[...] = kv
    for j in range(4):
        kv16_ref[j] = kv[:, HEAD_DIM * j:HEAD_DIM * (j + 1)].astype(BF16)
    cq = _rms_val(z[:, 768:1024], qn_ref[...]).astype(BF16)
    qmm = _mm(cq, wqu_ref[...])
    qlat = _mm(qmm[:, :512].astype(BF16), wuk_ref[...]) * MLA_SCALE
    for h in range(MLA_HEADS):
        qm_ref[h, :, 0:MLA_KV_LORA] = qlat[:, MLA_KV_LORA * h:MLA_KV_LORA * (h + 1)]
    for c in range(2):
        pe = _rope_chunk(qmm[:, 512 + LANES * c:512 + LANES * (c + 1)], rope_ref, MLA_ROPE // 2) * MLA_SCALE
        for hh in range(4):
            qm_ref[4 * c + hh, :, MLA_KV_LORA:MLA_LAT] = pe[:, MLA_ROPE * hh:MLA_ROPE * (hh + 1)]
    ckv = _rms_val(z[:, 1024:1152], kvn_ref[...])
    last = z[:, 1152:1280]
    kpe = _rope_chunk(last, rope_ref, MLA_ROPE // 2)[:, 0:MLA_ROPE]
    lat32_ref[:, 0:MLA_KV_LORA] = ckv
    lat32_ref[:, MLA_KV_LORA:MLA_LAT] = kpe
    lat16_ref[:, 0:MLA_KV_LORA] = ckv.astype(BF16)
    lat16_ref[:, MLA_KV_LORA:MLA_LAT] = kpe.astype(BF16)
    logf_ref[...] = _log_sigmoid(last + bf_ref[...])[:, EV_FLG_LANE:EV_FLG_LANE + FOX_HEADS]


def _even_proj(h, g, w_in, b_f, q_norm, w_q_up, wuk_bd, kv_norm, rope, tm, t_prompt, n_p):
    n = h.shape[0]
    npt = n_p // tm
    full = lambda shape: pl.BlockSpec(shape, lambda i: (0,) * len(shape))
    return pl.pallas_call(
        _even_proj_kernel,
        grid=(n // tm,),
        in_specs=[
            pl.BlockSpec((tm, D_MODEL), lambda i: (i, 0)),
            full((1, D_MODEL)), full((D_MODEL, EV_COLS)), full((1, LANES)), full((1, MLA_Q_LORA)),
            full((MLA_Q_LORA, 768)), full((512, 1024)), full((1, MLA_KV_LORA)),
            _rope_spec(tm, t_prompt, npt),
        ],
        out_specs=[
            pl.BlockSpec((FOX_HEADS, tm, HEAD_DIM), lambda i: (0, i, 0)),
            pl.BlockSpec((tm, KV_W), lambda i: (i, 0)),
            pl.BlockSpec((4, tm, HEAD_DIM), lambda i: (0, i, 0)),
            pl.BlockSpec((tm, FOX_HEADS), lambda i: (i, 0)),
            pl.BlockSpec((MLA_HEADS, tm, MLA_LAT), lambda i: (0, i, 0)),
            pl.BlockSpec((tm, MLA_LAT), lambda i: (i, 0)),
            pl.BlockSpec((tm, MLA_LAT), lambda i: (i, 0)),
        ],
        out_shape=[
            jax.ShapeDtypeStruct((FOX_HEADS, n, HEAD_DIM), F32),
            jax.ShapeDtypeStruct((n, KV_W), F32),
            jax.ShapeDtypeStruct((4, n, HEAD_DIM), BF16),
            jax.ShapeDtypeStruct((n, FOX_HEADS), F32),
            jax.ShapeDtypeStruct((MLA_HEADS, n, MLA_LAT), F32),
            jax.ShapeDtypeStruct((n, MLA_LAT), F32),
            jax.ShapeDtypeStruct((n, MLA_LAT), BF16),
        ],
        compiler_params=_cp(("parallel",)),
        name="even_proj",
    )(h, g, w_in, b_f, q_norm, w_q_up, wuk_bd, kv_norm, rope)


CS_CHUNK = 128


def _csum_kernel(lf_ref, cs_ref, cst_ref, *, t, tk):
    r = _iota((CS_CHUNK, CS_CHUNK), 0)
    c = _iota((CS_CHUNK, CS_CHUNK), 1)
    tri = (c <= r).astype(F32)
    eye = (_iota((FOX_HEADS, FOX_HEADS), 0) == _iota((FOX_HEADS, FOX_HEADS), 1)).astype(F32)
    carry = jnp.zeros((1, FOX_HEADS), F32)
    per = tk // CS_CHUNK
    for k in range(t // CS_CHUNK):
        lf = lf_ref[k * CS_CHUNK:(k + 1) * CS_CHUNK, :]
        cs = _mm(tri, lf, lax.Precision.HIGHEST) + carry
        carry = cs[CS_CHUNK - 1:CS_CHUNK, :]
        cs_ref[k * CS_CHUNK:(k + 1) * CS_CHUNK, :] = cs
        cst_ref[k // per, :, (k % per) * CS_CHUNK:(k % per + 1) * CS_CHUNK] = _nt(eye, cs, lax.Precision.HIGHEST)


def _csum(logf, b, t, tk):
    return pl.pallas_call(
        functools.partial(_csum_kernel, t=t, tk=tk),
        grid=(b,),
        in_specs=[pl.BlockSpec((t, FOX_HEADS), lambda i: (i, 0))],
        out_specs=[pl.BlockSpec((t, FOX_HEADS), lambda i: (i, 0)),
                   pl.BlockSpec((None, t // tk, FOX_HEADS, tk), lambda i: (i, 0, 0, 0))],
        out_shape=[jax.ShapeDtypeStruct((b * t, FOX_HEADS), F32),
                   jax.ShapeDtypeStruct((b, t // tk, FOX_HEADS, tk), F32)],
        compiler_params=_cp(("parallel",)),
        name="fox_csum",
    )(logf)


def _lane_chunks(s):
    return [s[:, LANES * c:LANES * (c + 1)] for c in range(s.shape[1] // LANES)]


def _online_update(chunks, m_ref, l_ref, acc_ref, v, idx):
    m_prev = m_ref[idx]
    mx = chunks[0]
    for c in chunks[1:]:
        mx = jnp.maximum(mx, c)
    m_new = jnp.maximum(m_prev, jnp.broadcast_to(jnp.max(mx, axis=-1, keepdims=True), m_prev.shape))
    alpha = jnp.exp(m_prev - m_new)
    ps = [jnp.exp(c - m_new) for c in chunks]
    lsum = ps[0]
    for p in ps[1:]:
        lsum = lsum + p
    l_ref[idx] = alpha * l_ref[idx] + lsum
    dv = acc_ref.shape[-1]
    acc_ref[idx] = alpha[:, :dv] * acc_ref[idx] + _mm(jnp.concatenate(ps, axis=1).astype(BF16), v)
    m_ref[idx] = m_new


def _online_finish(l_ref, acc_ref, idx):
    return acc_ref[idx] / jnp.sum(l_ref[idx], axis=-1, keepdims=True)


def _even_attn_kernel(qf_ref, qm_ref, cs_ref, kv_ref, lat_ref, cst_ref, wuv_ref, o_ref,
                      mf_ref, lf_ref, af_ref, mm_ref, lm_ref, am_ref, *, tq, tk):
    qi = pl.program_id(1)
    qs = qi * tq
    rq = FOX_HEADS // FOX_KV_HEADS
    nch = tk // LANES
    mf_ref[...] = jnp.full_like(mf_ref, NEG_INF)
    lf_ref[...] = jnp.zeros_like(lf_ref)
    af_ref[...] = jnp.zeros_like(af_ref)
    mm_ref[...] = jnp.full_like(mm_ref, NEG_INF)
    lm_ref[...] = jnp.zeros_like(lm_ref)
    am_ref[...] = jnp.zeros_like(am_ref)
    cs = cs_ref[...]
    cqb = [jnp.broadcast_to(cs[:, h:h + 1], (tq, LANES)) for h in range(FOX_HEADS)]
    qg = [qf_ref[rq * g:rq * (g + 1)].reshape(rq * tq, HEAD_DIM).astype(BF16) for g in range(FOX_KV_HEADS)]
    qm = qm_ref[...].reshape(MLA_HEADS * tq, MLA_LAT).astype(BF16)
    qpos = qs + _iota((tq, LANES), 0)

    def step(j, masked):
        ks = pl.multiple_of(j * tk, tk)
        if masked:
            keep = [(ks + LANES * c + _iota((tq, LANES), 1)) <= qpos for c in range(nch)]
        ck = cst_ref[j]
        for g in range(FOX_KV_HEADS):
            k = kv_ref[g, pl.ds(ks, tk), :]
            v = kv_ref[FOX_KV_HEADS + g, pl.ds(ks, tk), :]
            s = _nt(qg[g], k)
            chunks = []
            for c in range(nch):
                parts = []
                for r in range(rq):
                    h = rq * g + r
                    sr = s[r * tq:(r + 1) * tq, LANES * c:LANES * (c + 1)] + (
                        cqb[h] - ck[h:h + 1, LANES * c:LANES * (c + 1)])
                    parts.append(jnp.where(keep[c], sr, NEG_INF) if masked else sr)
                chunks.append(jnp.concatenate(parts, axis=0))
            _online_update(chunks, mf_ref, lf_ref, af_ref, v, g)
        lat = lat_ref[pl.ds(ks, tk), :]
        chunks = _lane_chunks(_nt(qm, lat))
        if masked:
            chunks = [jnp.where(keep[c][None], chunks[c].reshape(MLA_HEADS, tq, LANES), NEG_INF)
                      .reshape(MLA_HEADS * tq, LANES) for c in range(nch)]
        _online_update(chunks, mm_ref, lm_ref, am_ref, lat[:, :MLA_KV_LORA], 0)

    n_full = qs // tk
    n_all = (qs + tq - 1) // tk + 1

    def full_body(j, c):
        step(j, False)
        return c

    def diag_body(j, c):
        step(j, True)
        return c

    lax.fori_loop(0, n_full, full_body, 0)
    lax.fori_loop(n_full, n_all, diag_body, 0)

    for g in range(FOX_KV_HEADS):
        o = _online_finish(lf_ref, af_ref, g)
        for r in range(rq):
            h = rq * g + r
            o_ref[:, HEAD_DIM * h:HEAD_DIM * (h + 1)] = o[r * tq:(r + 1) * tq].astype(BF16)
    ol = _online_finish(lm_ref, am_ref, 0).astype(BF16)
    base = FOX_HEADS * HEAD_DIM
    for h in range(MLA_HEADS):
        om = _mm(ol[h * tq:(h + 1) * tq], wuv_ref[h])
        o_ref[:, base + MLA_V * h:base + MLA_V * (h + 1)] = om.astype(BF16)


def _even_attn(qf, qm, csum, kv16, lat16, cst, wuv, b, t, tq, tk):
    nq = t // tq
    rq = FOX_HEADS // FOX_KV_HEADS
    return pl.pallas_call(
        functools.partial(_even_attn_kernel, tq=tq, tk=tk),
        grid=(b, nq),
        in_specs=[
            pl.BlockSpec((FOX_HEADS, tq, HEAD_DIM), lambda i, j: (0, i * nq + j, 0)),
            pl.BlockSpec((MLA_HEADS, tq, MLA_LAT), lambda i, j: (0, i * nq + j, 0)),
            pl.BlockSpec((tq, FOX_HEADS), lambda i, j: (i * nq + j, 0)),
            pl.BlockSpec((4, t, HEAD_DIM), lambda i, j: (0, i, 0)),
            pl.BlockSpec((t, MLA_LAT), lambda i, j: (i, 0)),
            pl.BlockSpec((None, t // tk, FOX_HEADS, tk), lambda i, j: (i, 0, 0, 0)),
            pl.BlockSpec((MLA_HEADS, MLA_KV_LORA, MLA_V), lambda i, j: (0, 0, 0)),
        ],
        out_specs=pl.BlockSpec((tq, D_MODEL), lambda i, j: (i * nq + j, 0)),
        out_shape=jax.ShapeDtypeStruct((b * t, D_MODEL), BF16),
        scratch_shapes=[
            pltpu.VMEM((FOX_KV_HEADS, rq * tq, LANES), F32), pltpu.VMEM((FOX_KV_HEADS, rq * tq, LANES), F32),
            pltpu.VMEM((FOX_KV_HEADS, rq * tq, HEAD_DIM), F32),
            pltpu.VMEM((1, MLA_HEADS * tq, LANES), F32), pltpu.VMEM((1, MLA_HEADS * tq, LANES), F32),
            pltpu.VMEM((1, MLA_HEADS * tq, MLA_KV_LORA), F32),
        ],
        compiler_params=_cp(("parallel", "arbitrary")),
        name="even_attn_prompt",
    )(qf, qm, csum, kv16, lat16, cst, wuv)


def _split3(x):
    hi = x.astype(BF16)
    r1 = x - hi.astype(F32)
    mid = r1.astype(BF16)
    return hi, mid, (r1 - mid.astype(F32)).astype(BF16)


def _mm_exact_l(x, w):
    hi, mid, lo = _split3(x)
    return _mm(hi, w) + _mm(mid, w) + _mm(lo, w)


def _mm_exact_r(w, x):
    hi, mid, lo = _split3(x)
    return _mm(w, hi) + _mm(w, mid) + _mm(w, lo)


def _joint_softmax(s_past, s_new, pv_past, pv_new):
    mx = jnp.maximum(jnp.max(s_past, axis=-1, keepdims=True), jnp.max(s_new, axis=-1, keepdims=True))
    p1 = jnp.exp(s_past - mx)
    p2 = jnp.exp(s_new - mx)
    den = jnp.sum(p1, axis=-1, keepdims=True) + jnp.sum(p2, axis=-1, keepdims=True)
    return (pv_past(p1.astype(BF16)) + pv_new(p2.astype(BF16))) / den


def _even_dec_kernel(pt_ref, qf_ref, qm_ref, kvn_ref, latn_ref, lfn_ref, wuv_ref, tri_ref, ones_ref, pre_ref,
                     *rest, n_pages, td):
    kv_refs = rest[:n_pages]
    lf_refs = rest[n_pages:2 * n_pages]
    lat_refs = rest[2 * n_pages:3 * n_pages]
    o_ref = rest[3 * n_pages]
    rq = FOX_HEADS // FOX_KV_HEADS
    rows = rq * td

    lf = jnp.concatenate([lf_refs[i][...] for i in range(n_pages)], axis=0)
    cum = _mm_exact_l(lf, tri_ref[...])
    tot = _mm_exact_l(lf, ones_ref[...])
    pre = _mm_exact_r(pre_ref[...], tot)
    ck = cum + pre
    last = FOX_HEADS * (n_pages - 1)
    past_total = (pre + tot)[last:last + FOX_HEADS]

    tri_n = (_iota((td, td), 1) <= _iota((td, td), 0)).astype(F32)
    eye = (_iota((FOX_HEADS, FOX_HEADS), 0) == _iota((FOX_HEADS, FOX_HEADS), 1)).astype(F32)
    csn = _mm(tri_n, lfn_ref[...], lax.Precision.HIGHEST)
    ckn = _nt(eye, csn, lax.Precision.HIGHEST) + past_total[:, 0:td]
    kvn = kvn_ref[...].astype(BF16)
    keep = _iota((rows, td), 1) <= _iota((rows, td), 0) % td

    def rows_of(refs, lo, hi):
        return jnp.concatenate([refs[i][lo:hi, :].astype(BF16) for i in range(n_pages)], axis=1)

    for g in range(FOX_KV_HEADS):
        q = qf_ref[rq * g:rq * (g + 1)].reshape(rows, HEAD_DIM).astype(BF16)
        kt = rows_of(kv_refs, HEAD_DIM * g, HEAD_DIM * (g + 1))
        vo = (FOX_KV_HEADS + g) * HEAD_DIM
        vt = rows_of(kv_refs, vo, vo + HEAD_DIM)
        s = _mm(q, kt)
        parts = []
        for r in range(rq):
            h = rq * g + r
            bias = jnp.concatenate([jnp.broadcast_to(ck[FOX_HEADS * i + h:FOX_HEADS * i + h + 1, :], (td, PAGE))
                                    for i in range(n_pages)], axis=1)
            parts.append(s[r * td:(r + 1) * td] - bias)
        s = jnp.concatenate(parts, axis=0)
        sn = _nt(q, kvn[:, HEAD_DIM * g:HEAD_DIM * (g + 1)])
        sn = jnp.concatenate([sn[r * td:(r + 1) * td] - ckn[rq * g + r:rq * g + r + 1, :] for r in range(rq)], axis=0)
        sn = jnp.where(keep, sn, NEG_INF)
        o = _joint_softmax(s, sn, lambda p: _nt(p, vt), lambda p: _mm(p, kvn[:, vo:vo + HEAD_DIM]))
        for r in range(rq):
            h = rq * g + r
            o_ref[:, HEAD_DIM * h:HEAD_DIM * (h + 1)] = o[r * td:(r + 1) * td]

    latt = rows_of(lat_refs, 0, MLA_LAT)
    latn = latn_ref[...].astype(BF16)
    qm = qm_ref[...].reshape(MLA_HEADS * td, MLA_LAT).astype(BF16)
    keep_m = _iota((MLA_HEADS * td, td), 1) <= _iota((MLA_HEADS * td, td), 0) % td
    sn = jnp.where(keep_m, _nt(qm, latn), NEG_INF)
    ol = _joint_softmax(_mm(qm, latt), sn, lambda p: _nt(p, latt[:MLA_KV_LORA]),
                        lambda p: _mm(p, latn[:, :MLA_KV_LORA])).astype(BF16)
    base = FOX_HEADS * HEAD_DIM
    for h in range(MLA_HEADS):
        o_ref[:, base + MLA_V * h:base + MLA_V * (h + 1)] = _mm(ol[h * td:(h + 1) * td], wuv_ref[h])


def _even_dec(page_table, qf, qm, kv32, lat32, logf, wuv, cache_kv, cache_lft, cache_lat, e, n_p, bd, td):
    n_pages = page_table.shape[1]
    off = n_p // td
    nr = n_pages * FOX_HEADS
    tri = jnp.asarray(np.triu(np.ones((PAGE, PAGE), np.float32)), dtype=BF16)
    ones = jnp.ones((PAGE, PAGE), BF16)
    r = np.arange(nr)
    pre = ((r[None, :] // FOX_HEADS < r[:, None] // FOX_HEADS) & (r[None, :] % FOX_HEADS == r[:, None] % FOX_HEADS))
    pre = jnp.asarray(pre.astype(np.float32), dtype=BF16)

    def page_spec(shape, i):
        return pl.BlockSpec((None, None) + shape, lambda b, pt: (e, pt[b, i], 0, 0))

    full = lambda shape: pl.BlockSpec(shape, lambda b, pt: (0,) * len(shape))
    in_specs = [
        pl.BlockSpec((FOX_HEADS, td, HEAD_DIM), lambda b, pt: (0, off + b, 0)),
        pl.BlockSpec((MLA_HEADS, td, MLA_LAT), lambda b, pt: (0, off + b, 0)),
        pl.BlockSpec((td, KV_W), lambda b, pt: (off + b, 0)),
        pl.BlockSpec((td, MLA_LAT), lambda b, pt: (off + b, 0)),
        pl.BlockSpec((td, FOX_HEADS), lambda b, pt: (off + b, 0)),
        full((MLA_HEADS, MLA_KV_LORA, MLA_V)), full((PAGE, PAGE)), full((PAGE, PAGE)), full((nr, nr)),
    ]
    in_specs += [page_spec((KV_W, PAGE), i) for i in range(n_pages)]
    in_specs += [page_spec((FOX_HEADS, PAGE), i) for i in range(n_pages)]
    in_specs += [page_spec((MLA_LAT, PAGE), i) for i in range(n_pages)]
    grid_spec = pltpu.PrefetchScalarGridSpec(
        num_scalar_prefetch=1,
        grid=(bd,),
        in_specs=in_specs,
        out_specs=pl.BlockSpec((td, D_MODEL), lambda b, pt: (b, 0)),
    )
    return pl.pallas_call(
        functools.partial(_even_dec_kernel, n_pages=n_pages, td=td),
        grid_spec=grid_spec,
        out_shape=jax.ShapeDtypeStruct((bd * td, D_MODEL), F32),
        compiler_params=_cp(("parallel",)),
        name="even_attn_decode",
    )(page_table, qf, qm, kv32, lat32, logf, wuv, tri, ones, pre,
      *([cache_kv] * n_pages), *([cache_lft] * n_pages), *([cache_lat] * n_pages))


OD_COLS = 1920
N_GATES = 3 * NSA_HEADS


def _odd_proj_kernel(x_ref, g_ref, win_ref, bg_ref, rope_ref,
                     q_ref, cmp_ref, slc_ref, wn_ref, slc16_ref, wn16_ref, gate_ref):
    xn = _rms_val(x_ref[...], g_ref[...]).astype(BF16)
    z = _mm(xn, win_ref[...])
    half = ROPE_DIM // 2
    for c in range(NSA_HEADS // 2):
        qc = _rope_chunk(z[:, LANES * c:LANES * (c + 1)], rope_ref, half) * FOX_SCALE
        q_ref[2 * c] = qc[:, :HEAD_DIM]
        q_ref[2 * c + 1] = qc[:, HEAD_DIM:]
    ko, vo = NSA_HEADS * HEAD_DIM, NSA_HEADS * HEAD_DIM + 3 * LANES
    for n, (r32, r16) in enumerate(((cmp_ref, None), (slc_ref, slc16_ref), (wn_ref, wn16_ref))):
        k = _rope_chunk(z[:, ko + LANES * n:ko + LANES * (n + 1)], rope_ref, half)
        v = z[:, vo + LANES * n:vo + LANES * (n + 1)]
        r32[:, 0:LANES] = k
        r32[:, LANES:2 * LANES] = v
        if r16 is not None:
            for g in range(NSA_KV_HEADS):
                r16[g] = k[:, HEAD_DIM * g:HEAD_DIM * (g + 1)].astype(BF16)
                r16[NSA_KV_HEADS + g] = v[:, HEAD_DIM * g:HEAD_DIM * (g + 1)].astype(BF16)
    gate_ref[...] = jax.nn.sigmoid(z[:, vo + 3 * LANES:vo + 4 * LANES] + bg_ref[...])[:, 0:N_GATES]


def _odd_proj(h, g, w_in, b_gate, rope, tm, t_prompt, n_p):
    n = h.shape[0]
    npt = n_p // tm
    full = lambda shape: pl.BlockSpec(shape, lambda i: (0,) * len(shape))
    row = lambda w: pl.BlockSpec((tm, w), lambda i: (i, 0))
    hm = lambda k: pl.BlockSpec((k, tm, HEAD_DIM), lambda i: (0, i, 0))
    return pl.pallas_call(
        _odd_proj_kernel,
        grid=(n // tm,),
        in_specs=[row(D_MODEL), full((1, D_MODEL)), full((D_MODEL, OD_COLS)), full((1, LANES)),
                  _rope_spec(tm, t_prompt, npt)],
        out_specs=[hm(NSA_HEADS), row(KV_W), row(KV_W), row(KV_W), hm(4), hm(4), row(N_GATES)],
        out_shape=[
            jax.ShapeDtypeStruct((NSA_HEADS, n, HEAD_DIM), F32),
            jax.ShapeDtypeStruct((n, KV_W), F32), jax.ShapeDtypeStruct((n, KV_W), F32),
            jax.ShapeDtypeStruct((n, KV_W), F32),
            jax.ShapeDtypeStruct((4, n, HEAD_DIM), BF16), jax.ShapeDtypeStruct((4, n, HEAD_DIM), BF16),
            jax.ShapeDtypeStruct((n, N_GATES), F32),
        ],
        compiler_params=_cp(("parallel",)),
        name="odd_proj",
    )(h, g, w_in, b_gate, rope)


def _compress_rows(ch, pe_ref, wlo_ref, whi_ref, w2_ref):
    a = _mm((ch + pe_ref[0:1, :]).astype(BF16), wlo_ref[...])
    b = _mm((ch + pe_ref[1:2, :]).astype(BF16), whi_ref[...])
    hid = a + pltpu.roll(b, ch.shape[0] - 1, 0)
    return _mm(_gelu_tanh(hid).astype(BF16), w2_ref[...])


def _compress_prompt_kernel(ch_ref, pe_ref, wlo_ref, whi_ref, w2_ref, o_ref):
    o_ref[...] = _compress_rows(ch_ref[...], pe_ref, wlo_ref, whi_ref, w2_ref).astype(BF16)


def _compress_prompt(chunks, pe, wlo, whi, w2, b, rows):
    full = lambda shape: pl.BlockSpec(shape, lambda i: (0,) * len(shape))
    return pl.pallas_call(
        _compress_prompt_kernel,
        grid=(b,),
        in_specs=[pl.BlockSpec((rows, CHUNK_W), lambda i: (i, 0)),
                  full((2, CHUNK_W)), full((CHUNK_W, KV_W)), full((CHUNK_W, KV_W)), full((KV_W, KV_W))],
        out_specs=pl.BlockSpec((rows, KV_W), lambda i: (i, 0)),
        out_shape=jax.ShapeDtypeStruct((b * rows, KV_W), BF16),
        compiler_params=_cp(("parallel",)),
        name="nsa_compress_prompt",
    )(chunks, pe, wlo, whi, w2)


def _cmp_branch(q, kc, vc, qpos_col, heads, t):
    ncp = kc.shape[0]
    s = _nt(q, kc).reshape(heads, t, ncp)
    mc = (_iota((t, ncp), 1) * CMP_D + (CMP_L - 1)) <= qpos_col
    s = jnp.where(mc[None], s, NEG_INF)
    e = jnp.exp(s - jnp.max(s, axis=-1, keepdims=True))
    p = jnp.where(mc[None], e / jnp.sum(e, axis=-1, keepdims=True), 0.0)
    o = _mm(p.reshape(heads * t, ncp).astype(BF16), vc)
    return o, jnp.sum(p, axis=0)


def _select_blocks(imp_t, qpos_row, sc_ref, n_real):
    ns, c = imp_t.shape
    blk = _iota((ns, c), 0)
    cur = qpos_row // SEL_L
    forced = (blk == 0) | (blk == cur) | (blk == cur - 1)
    valid = (blk * SEL_L <= qpos_row) & (blk < n_real)
    score = jnp.where(valid, jnp.where(forced, FORCE_SCORE, imp_t), -1.0)
    score = jnp.where(blk < n_real, score, -2.0)
    sc_ref[...] = score

    def body(i, rank):
        row = sc_ref[pl.ds(i, 1), :]
        ahead = (row > score) | ((row == score) & (i < blk))
        return rank + ahead.astype(F32)

    rank = lax.fori_loop(0, n_real, body, jnp.zeros((ns, c), F32))
    return (rank < float(min(TOP_N, n_real))).astype(F32)


def _select_blocks_rows(imp, qpos_col, n_real):
    c, ns = imp.shape
    blk = _iota((c, ns), 1)
    cur = qpos_col // SEL_L
    forced = (blk == 0) | (blk == cur) | (blk == cur - 1)
    valid = (blk * SEL_L <= qpos_col) & (blk < n_real)
    score = jnp.where(valid, jnp.where(forced, FORCE_SCORE, imp), -1.0)
    score = jnp.where(blk < n_real, score, -2.0)
    rank = jnp.zeros((c, ns), F32)
    for i in range(n_real):
        col = score[:, i:i + 1]
        ahead = (col > score) | ((col == score) & (blk > i))
        rank = rank + ahead.astype(F32)
    return (rank < float(min(TOP_N, n_real))).astype(F32)


def _imp_matrix(ns_pad, ncp):
    ratio, span = SEL_L // CMP_D, CMP_L // CMP_D
    m = np.zeros((ns_pad, ncp), np.float32)
    for j in range(ns_pad):
        for a in range(ratio):
            for b in range(span):
                cc = j * ratio + a + b
                if cc < ncp:
                    m[j, cc] += 1.0
    return jnp.asarray(m)


def _expand_matrix(ns_pad, n_keys):
    m = (np.arange(n_keys)[None, :] // SEL_L == np.arange(ns_pad)[:, None]).astype(np.float32)
    return jnp.asarray(m, dtype=BF16)


def _nsa_prompt_kernel(q_ref, gate_ref, kvc_ref, slc_ref, wn_ref, mimp_ref, eye_ref, exp_ref, o_ref,
                       m_ref, l_ref, acc_ref, mask_ref, sc_ref, *, tq, tk, t):
    qi = pl.program_id(1)
    qs = qi * tq
    rq = NSA_HEADS // NSA_KV_HEADS
    ns = t // SEL_L
    qpos_col = qs + _iota((tq, 1), 0)
    qpos_row = qs + _iota((1, tq), 1)
    kvc = kvc_ref[...]
    gates = gate_ref[...]
    nch = tk // LANES
    causal = (_iota((tq, t), 1) <= qs + _iota((tq, t), 0)).astype(F32)
    wspan = min(WINDOW + tq, t)
    w0 = pl.multiple_of(jnp.minimum(jnp.maximum(qs - WINDOW, 0), t - wspan), tq)
    dwin = (qs + _iota((tq, wspan), 0)) - (w0 + _iota((tq, wspan), 1))
    keep_win = (dwin >= 0) & (dwin < WINDOW)

    def masked_chunks(s, keep):
        out = []
        for c in range(s.shape[1] // LANES):
            sl = slice(LANES * c, LANES * (c + 1))
            out.append(jnp.where(keep[:, sl][None], s[:, sl].reshape(rq, tq, LANES), NEG_INF)
                       .reshape(rq * tq, LANES))
        return out

    for g in range(NSA_KV_HEADS):
        q = q_ref[rq * g:rq * (g + 1)].reshape(rq * tq, HEAD_DIM).astype(BF16)
        kc = kvc[:, HEAD_DIM * g:HEAD_DIM * (g + 1)]
        vc = kvc[:, HEAD_DIM * (NSA_KV_HEADS + g):HEAD_DIM * (NSA_KV_HEADS + g + 1)]
        o_cmp, pgrp = _cmp_branch(q, kc, vc, qpos_col, rq, tq)
        imp_t = _nt(mimp_ref[...], pgrp, lax.Precision.HIGHEST)
        sel_t = _select_blocks(imp_t, qpos_row, sc_ref, ns)
        sel = _nt(eye_ref[...], sel_t.astype(BF16))
        mfull = _mm(sel.astype(BF16), exp_ref[...]) * causal
        for jj in range(t // tk):
            mask_ref[jj] = mfull[:, jj * tk:(jj + 1) * tk]

        m_ref[...] = jnp.full_like(m_ref, NEG_INF)
        l_ref[...] = jnp.zeros_like(l_ref)
        acc_ref[...] = jnp.zeros_like(acc_ref)

        def slc_body(j, c):
            ks = pl.multiple_of(j * tk, tk)
            k = slc_ref[g, pl.ds(ks, tk), :]
            v = slc_ref[NSA_KV_HEADS + g, pl.ds(ks, tk), :]
            _online_update(masked_chunks(_nt(q, k), mask_ref[j] > 0.5), m_ref, l_ref, acc_ref, v, 0)
            return c

        n_all = (qs + tq - 1) // tk + 1
        lax.fori_loop(0, n_all, slc_body, 0)
        o_slc = _online_finish(l_ref, acc_ref, 0)

        kw = wn_ref[g, pl.ds(w0, wspan), :]
        vw = wn_ref[NSA_KV_HEADS + g, pl.ds(w0, wspan), :]
        chunks = masked_chunks(_nt(q, kw), keep_win)
        mx = chunks[0]
        for c in chunks[1:]:
            mx = jnp.maximum(mx, c)
        mx = jnp.broadcast_to(jnp.max(mx, axis=-1, keepdims=True), mx.shape)
        ps = [jnp.exp(c - mx) for c in chunks]
        den = ps[0]
        for p in ps[1:]:
            den = den + p
        o_win = _mm(jnp.concatenate(ps, axis=1).astype(BF16), vw) / jnp.sum(den, axis=-1, keepdims=True)

        for r in range(rq):
            h = rq * g + r
            rows = slice(r * tq, (r + 1) * tq)
            o = (gates[:, h:h + 1] * o_cmp[rows] + gates[:, NSA_HEADS + h:NSA_HEADS + h + 1] * o_slc[rows]
                 + gates[:, 2 * NSA_HEADS + h:2 * NSA_HEADS + h + 1] * o_win[rows])
            o_ref[:, HEAD_DIM * h:HEAD_DIM * (h + 1)] = o.astype(BF16)


def _nsa_prompt(q, gates, kvc, slc16, wn16, mimp, eye, expand, b, t, tq, tk):
    nq = t // tq
    rq = NSA_HEADS // NSA_KV_HEADS
    ncp = t // CMP_D
    ns = t // SEL_L
    return pl.pallas_call(
        functools.partial(_nsa_prompt_kernel, tq=tq, tk=tk, t=t),
        grid=(b, nq),
        in_specs=[
            pl.BlockSpec((NSA_HEADS, tq, HEAD_DIM), lambda i, j: (0, i * nq + j, 0)),
            pl.BlockSpec((tq, N_GATES), lambda i, j: (i * nq + j, 0)),
            pl.BlockSpec((ncp, KV_W), lambda i, j: (i, 0)),
            pl.BlockSpec((4, t, HEAD_DIM), lambda i, j: (0, i, 0)),
            pl.BlockSpec((4, t, HEAD_DIM), lambda i, j: (0, i, 0)),
            pl.BlockSpec((ns, ncp), lambda i, j: (0, 0)),
            pl.BlockSpec((tq, tq), lambda i, j: (0, 0)),
            pl.BlockSpec((ns, t), lambda i, j: (0, 0)),
        ],
        out_specs=pl.BlockSpec((tq, D_MODEL), lambda i, j: (i * nq + j, 0)),
        out_shape=jax.ShapeDtypeStruct((b * t, D_MODEL), BF16),
        scratch_shapes=[
            pltpu.VMEM((1, rq * tq, LANES), F32), pltpu.VMEM((1, rq * tq, LANES), F32),
            pltpu.VMEM((1, rq * tq, HEAD_DIM), F32),
            pltpu.VMEM((t // tk, tq, tk), F32),
            pltpu.VMEM((ns, tq), F32),
        ],
        compiler_params=_cp(("parallel", "arbitrary")),
        name="nsa_attn_prompt",
    )(q, gates, kvc, slc16, wn16, mimp, eye, expand)


def _nsa_dec1_kernel(pt_ref, q_ref, pe_ref, wlo_ref, whi_ref, w2_ref, mimp_ref, exp_ref, *rest,
                     n_pages, td, past_len, ns_pad):
    pages = rest[:n_pages]
    ocmp_ref, mask_ref, xs_ref = rest[n_pages:]
    rq = NSA_HEADS // NSA_KV_HEADS
    n_real = -(-(past_len + td) // SEL_L)
    nchunk = past_len // CMP_D
    for i in range(n_pages):
        x = pages[i][...].T
        xs_ref[0, PAGE * i:PAGE * (i + 1), :] = x[:, :LANES]
        xs_ref[1, PAGE * i:PAGE * (i + 1), :] = x[:, LANES:]
    a = jnp.zeros((nchunk, KV_W), F32)
    b = jnp.zeros((nchunk, KV_W), F32)
    for l in range(CMP_D):
        r = jnp.concatenate([xs_ref[0, pl.ds(l, nchunk, stride=CMP_D), :],
                             xs_ref[1, pl.ds(l, nchunk, stride=CMP_D), :]], axis=1)
        ws = slice(KV_W * l, KV_W * (l + 1))
        a = a + _mm((r + pe_ref[0:1, ws]).astype(BF16), wlo_ref[ws, :])
        b = b + _mm((r + pe_ref[1:2, ws]).astype(BF16), whi_ref[ws, :])
    hid = a + pltpu.roll(b, nchunk - 1, 0)
    kvc = _mm(_gelu_tanh(hid).astype(BF16), w2_ref[...]).astype(BF16)
    qpos_col = past_len + _iota((td, 1), 0)
    pgs = []
    for g in range(NSA_KV_HEADS):
        q = q_ref[rq * g:rq * (g + 1)].reshape(rq * td, HEAD_DIM).astype(BF16)
        kc = kvc[:, HEAD_DIM * g:HEAD_DIM * (g + 1)]
        vc = kvc[:, HEAD_DIM * (NSA_KV_HEADS + g):HEAD_DIM * (NSA_KV_HEADS + g + 1)]
        o_cmp, pgrp = _cmp_branch(q, kc, vc, qpos_col, rq, td)
        for r in range(rq):
            h = rq * g + r
            ocmp_ref[:, HEAD_DIM * h:HEAD_DIM * (h + 1)] = o_cmp[r * td:(r + 1) * td]
        pgs.append(pgrp)
    pg_all = jnp.concatenate(pgs, axis=0)
    imp = _mm(pg_all, mimp_ref[...], lax.Precision.HIGHEST)
    qpos_sel = past_len + _iota((NSA_KV_HEADS * td, 1), 0) % td
    sel = _select_blocks_rows(imp, qpos_sel, n_real)
    mask_ref[...] = _mm(sel.astype(BF16), exp_ref[...]).astype(BF16)


def _nsa_dec1(page_table, q, pe, wlo, whi, w2, mimp, expand, cache_chunks, e, n_p, bd, td, ns_pad):
    n_pages = page_table.shape[1]
    past_len = n_pages * PAGE
    off = n_p // td
    ncp = past_len // CMP_D
    full = lambda shape: pl.BlockSpec(shape, lambda b, pt: (0,) * len(shape))
    in_specs = [
        pl.BlockSpec((NSA_HEADS, td, HEAD_DIM), lambda b, pt: (0, off + b, 0)),
        full((2, CHUNK_W)), full((CHUNK_W, KV_W)), full((CHUNK_W, KV_W)), full((KV_W, KV_W)),
        full((ncp, ns_pad)), full((ns_pad, past_len)),
    ]
    in_specs += [pl.BlockSpec((None, None, KV_W, PAGE), functools.partial(
        lambda b, pt, i: (e, pt[b, i], 0, 0), i=i)) for i in range(n_pages)]
    grid_spec = pltpu.PrefetchScalarGridSpec(
        num_scalar_prefetch=1,
        grid=(bd,),
        in_specs=in_specs,
        out_specs=[pl.BlockSpec((td, D_MODEL), lambda b, pt: (b, 0)),
                   pl.BlockSpec((None, NSA_KV_HEADS * td, past_len), lambda b, pt: (b, 0, 0))],
        scratch_shapes=[pltpu.VMEM((2, past_len, LANES), F32)],
    )
    return pl.pallas_call(
        functools.partial(_nsa_dec1_kernel, n_pages=n_pages, td=td, past_len=past_len, ns_pad=ns_pad),
        grid_spec=grid_spec,
        out_shape=[jax.ShapeDtypeStruct((bd * td, D_MODEL), F32),
                   jax.ShapeDtypeStruct((bd, NSA_KV_HEADS * td, past_len), BF16)],
        compiler_params=_cp(("parallel",)),
        name="nsa_decode_select",
    )(page_table, q, pe, wlo, whi, w2, mimp, expand, *([cache_chunks] * n_pages))


def _nsa_dec2_kernel(pt_ref, q_ref, gate_ref, ocmp_ref, mask_ref, slcn_ref, wnn_ref, state_ref, *rest,
                     n_pages, td, past_len):
    pages = rest[:n_pages]
    o_ref, nst_ref = rest[n_pages:]
    rq = NSA_HEADS // NSA_KV_HEADS
    rows = rq * td
    wbuf = state_ref.shape[1]

    def rows_of(lo, hi):
        return jnp.concatenate([pages[i][lo:hi, :].astype(BF16) for i in range(n_pages)], axis=1)

    qgs = [q_ref[rq * g:rq * (g + 1)].reshape(rows, HEAD_DIM).astype(BF16) for g in range(NSA_KV_HEADS)]
    msk = mask_ref[...].astype(F32)

    gates = gate_ref[...]
    ocmp = ocmp_ref[...]
    slcn = slcn_ref[...].astype(BF16)
    wnn = wnn_ref[...]
    st = state_ref[...]
    eye = (_iota((KV_W, KV_W), 0) == _iota((KV_W, KV_W), 1)).astype(F32)
    place = (_iota((td, wbuf), 1) == _iota((td, wbuf), 0) + (wbuf - td)).astype(F32)
    wnn_t = _nt(eye, wnn, lax.Precision.HIGHEST)
    tail = _mm(wnn_t, place, lax.Precision.HIGHEST)
    nst_ref[...] = jnp.where(_iota((KV_W, wbuf), 1) < wbuf - td, pltpu.roll(st, wbuf - td, 1), tail)
    st16 = st.astype(BF16)
    wnn16 = wnn.astype(BF16)
    keep_new = _iota((rows, td), 1) <= _iota((rows, td), 0) % td
    dpos = (past_len + _iota((rows, wbuf), 0) % td) - (past_len - wbuf + _iota((rows, wbuf), 1))
    keep_st = (dpos >= 0) & (dpos < WINDOW)
    for g in range(NSA_KV_HEADS):
        ko, vo = HEAD_DIM * g, HEAD_DIM * (NSA_KV_HEADS + g)
        kt = rows_of(ko, ko + HEAD_DIM)
        vt = rows_of(vo, vo + HEAD_DIM)
        keep = jnp.broadcast_to((msk[td * g:td * (g + 1)] > 0.5)[None], (rq, td, past_len)).reshape(rows, past_len)
        s = jnp.where(keep, _mm(qgs[g], kt), NEG_INF)
        sn = jnp.where(keep_new, _nt(qgs[g], slcn[:, ko:ko + HEAD_DIM]), NEG_INF)
        o_slc = _joint_softmax(s, sn, lambda p: _nt(p, vt), lambda p: _mm(p, slcn[:, vo:vo + HEAD_DIM]))
        s1 = jnp.where(keep_st, _mm(qgs[g], st16[ko:ko + HEAD_DIM]), NEG_INF)
        s2 = jnp.where(keep_new, _nt(qgs[g], wnn16[:, ko:ko + HEAD_DIM]), NEG_INF)
        o_win = _joint_softmax(s1, s2, lambda p: _nt(p, st16[vo:vo + HEAD_DIM]),
                               lambda p: _mm(p, wnn16[:, vo:vo + HEAD_DIM]))
        for r in range(rq):
            h = rq * g + r
            rs = slice(r * td, (r + 1) * td)
            cols = slice(HEAD_DIM * h, HEAD_DIM * (h + 1))
            o_ref[:, cols] = (gates[:, h:h + 1] * ocmp[:, cols]
                              + gates[:, NSA_HEADS + h:NSA_HEADS + h + 1] * o_slc[rs]
                              + gates[:, 2 * NSA_HEADS + h:2 * NSA_HEADS + h + 1] * o_win[rs])


def _nsa_dec2(page_table, q, gates, ocmp, mask, slc32, wn32, state, cache_slc, e, n_p, bd, td):
    n_pages = page_table.shape[1]
    past_len = n_pages * PAGE
    off = n_p // td
    wbuf = state.shape[3]
    in_specs = [
        pl.BlockSpec((NSA_HEADS, td, HEAD_DIM), lambda b, pt: (0, off + b, 0)),
        pl.BlockSpec((td, N_GATES), lambda b, pt: (off + b, 0)),
        pl.BlockSpec((td, D_MODEL), lambda b, pt: (b, 0)),
        pl.BlockSpec((None, NSA_KV_HEADS * td, past_len), lambda b, pt: (b, 0, 0)),
        pl.BlockSpec((td, KV_W), lambda b, pt: (off + b, 0)),
        pl.BlockSpec((td, KV_W), lambda b, pt: (off + b, 0)),
        pl.BlockSpec((None, None, KV_W, wbuf), lambda b, pt: (e, b, 0, 0)),
    ]
    in_specs += [pl.BlockSpec((None, None, KV_W, PAGE), functools.partial(
        lambda b, pt, i: (e, pt[b, i], 0, 0), i=i)) for i in range(n_pages)]
    grid_spec = pltpu.PrefetchScalarGridSpec(
        num_scalar_prefetch=1,
        grid=(bd,),
        in_specs=in_specs,
        out_specs=[pl.BlockSpec((td, D_MODEL), lambda b, pt: (b, 0)),
                   pl.BlockSpec((None, KV_W, wbuf), lambda b, pt: (b, 0, 0))],
    )
    return pl.pallas_call(
        functools.partial(_nsa_dec2_kernel, n_pages=n_pages, td=td, past_len=past_len),
        grid_spec=grid_spec,
        out_shape=[jax.ShapeDtypeStruct((bd * td, D_MODEL), F32),
                   jax.ShapeDtypeStruct((bd, KV_W, wbuf), F32)],
        compiler_params=_cp(("parallel",)),
        name="nsa_decode_attend",
    )(page_table, q, gates, ocmp, mask, slc32, wn32, state, *([cache_slc] * n_pages))


def _prep_even(w_in, b_f, w_q_up, w_uk, w_uv):
    pad = jnp.zeros((D_MODEL, EV_COLS - 1192), F32)
    w = jnp.concatenate([w_in[:, 0:768], w_in[:, 776:1192], w_in[:, 768:776], pad], axis=1).astype(BF16)
    bf = jnp.zeros((1, LANES), F32).at[0, EV_FLG_LANE:EV_FLG_LANE + FOX_HEADS].set(b_f)
    wq = w_q_up.reshape(MLA_Q_LORA, MLA_HEADS, MLA_QK)
    wq = jnp.concatenate([wq[:, :, :MLA_NOPE].reshape(MLA_Q_LORA, -1),
                          wq[:, :, MLA_NOPE:].reshape(MLA_Q_LORA, -1)], axis=1).astype(BF16)
    wuk_h = jnp.transpose(w_uk, (1, 2, 0))
    eye = jnp.eye(MLA_HEADS, dtype=F32)
    wuk_bd = (eye[:, None, :, None] * wuk_h[:, :, None, :]).reshape(MLA_HEADS * MLA_NOPE, MLA_HEADS * MLA_KV_LORA)
    wuv = jnp.transpose(w_uv, (1, 0, 2)).astype(BF16)
    return w, bf, wq, wuk_bd.astype(BF16), wuv


def _prep_odd(w_in, b_gate, pe, w1, w2):
    qw = NSA_HEADS * HEAD_DIM
    seg = lambda n: w_in[:, qw + LANES * n:qw + LANES * (n + 1)]
    gcol = w_in[:, qw + 6 * LANES:qw + 6 * LANES + N_GATES]
    perm = np.array([h * 3 + c for c in range(3) for h in range(NSA_HEADS)])
    pad = jnp.zeros((D_MODEL, LANES - N_GATES), F32)
    w = jnp.concatenate([w_in[:, :qw], seg(0), seg(2), seg(4), seg(1), seg(3), seg(5), gcol[:, perm], pad],
                        axis=1).astype(BF16)
    bg = jnp.zeros((1, LANES), F32).at[0, 0:N_GATES].set(b_gate[perm])
    kv_of = (0, 0, 1, 1)
    pe_rows, w_halves = [], []
    eye4 = jnp.eye(4, dtype=F32)
    for half in range(2):
        ls = slice(half * CMP_D, (half + 1) * CMP_D)
        pe_rows.append(jnp.stack([pe[kv_of[j], ls, :] for j in range(4)], axis=1).reshape(CHUNK_W))
        w1r = jnp.stack([w1[kv_of[j]].reshape(CMP_L, HEAD_DIM, HEAD_DIM)[ls] for j in range(4)], axis=1)
        wexp = (w1r[:, :, :, None, :] * eye4[None, :, None, :, None]).reshape(CHUNK_W, KV_W)
        w_halves.append(wexp.astype(BF16))
    w2bd = (eye4[:, None, :, None] * jnp.stack([w2[kv_of[j]] for j in range(4)])[:, :, None, :]).reshape(KV_W, KV_W)
    return w, bg, jnp.stack(pe_rows), w_halves[0], w_halves[1], w2bd.astype(BF16)


def kernel(x_prompt, x_sample, cache_fox_kv, cache_fox_logf, cache_mla, cache_nsa_cmp, cache_nsa_slc,
           state_nsa_win, page_table, norm_w, final_norm, ffn_w_gate, ffn_w_up, ffn_w_down,
           ev_w_in, ev_b_f, mla_q_norm, mla_w_q_up, mla_kv_norm, mla_w_uk, mla_w_uv, ev_w_out,
           od_w_in, od_b_gate, nsa_cmp_pe, nsa_cmp_w1, nsa_cmp_w2, od_w_out):
    b, t, _ = x_prompt.shape
    bd, td, _ = x_sample.shape
    depth = norm_w.shape[0]
    n_pages = page_table.shape[1]
    past_len = n_pages * PAGE
    n_p, n_s = b * t, bd * td
    n = n_p + n_s
    wbuf = state_nsa_win.shape[2]

    tm = math.gcd(512, n_s)
    tf = FF_DIM // 2
    tm_ffn = tm
    tq = min(256, t)
    tk = min(512, t)
    ns_pad = -(-(-(-(past_len + td) // SEL_L)) // LANES) * LANES

    rope_mla = _rope_tables(t, past_len, td, tm, MLA_THETA, MLA_ROPE, MLA_ROPE)
    rope_nsa = _rope_tables(t, past_len, td, tm, ROPE_THETA, ROPE_DIM, HEAD_DIM)
    mimp_p = _imp_matrix(t // SEL_L, t // CMP_D)
    mimp_s = _imp_matrix(ns_pad, past_len // CMP_D).T
    expand_p = _expand_matrix(t // SEL_L, t)
    expand_s = _expand_matrix(ns_pad, past_len)
    eye_q = jnp.eye(tq, dtype=BF16)

    wg16, wu16, wd16 = ffn_w_gate.astype(BF16), ffn_w_up.astype(BF16), ffn_w_down.astype(BF16)
    tok_minor = lambda c: jnp.transpose(c, (0, 1, 3, 4, 5, 2)).reshape(c.shape[0], c.shape[1], KV_W, c.shape[2])
    cache_kv2 = tok_minor(cache_fox_kv)
    cache_lft = jnp.swapaxes(cache_fox_logf, 2, 3)
    cache_lat2 = jnp.swapaxes(cache_mla, 2, 3)
    cache_cmp2 = tok_minor(cache_nsa_cmp)
    cache_slc2 = tok_minor(cache_nsa_slc)
    state2 = tok_minor(state_nsa_win)

    h = jnp.concatenate([x_prompt.reshape(n_p, D_MODEL), x_sample.reshape(n_s, D_MODEL)], axis=0)
    row = lambda v: v.reshape(1, -1)
    kv_shape = (2, FOX_KV_HEADS, HEAD_DIM)
    outs = {k: [] for k in ("fkv_p", "fkv_s", "flf_p", "flf_s", "mla_p", "mla_s",
                            "cmp_p", "cmp_s", "slc_p", "slc_s", "win_p", "win_s")}
    for li in range(depth):
        e = li // 2
        h = _ffn(h, row(norm_w[li, 0]), wg16[li, 0], wu16[li, 0], wd16[li, 0], tm_ffn, tf)
        if li % 2 == 0:
            w, bf, wq, wuk_bd, wuv = _prep_even(ev_w_in[e], ev_b_f[e], mla_w_q_up[e], mla_w_uk[e], mla_w_uv[e])
            qf, kv32, kv16, logf, qm, lat32, lat16 = _even_proj(
                h, row(norm_w[li, 1]), w, bf, row(mla_q_norm[e]), wq, wuk_bd, row(mla_kv_norm[e]),
                rope_mla, tm, t, n_p)
            csum, cst = _csum(logf, b, t, tk)
            o_p = _even_attn(qf, qm, csum, kv16, lat16, cst, wuv, b, t, tq, tk)
            o_s = _even_dec(page_table, qf, qm, kv32, lat32, logf, wuv, cache_kv2, cache_lft, cache_lat2,
                            e, n_p, bd, td)
            h = _out_proj(h, o_p, o_s, ev_w_out[e].astype(BF16), tm)
            outs["fkv_p"].append(kv32[:n_p].reshape((b, t) + kv_shape))
            outs["fkv_s"].append(kv32[n_p:].reshape((bd, td) + kv_shape))
            outs["flf_p"].append(logf[:n_p].reshape(b, t, FOX_HEADS))
            outs["flf_s"].append(logf[n_p:].reshape(bd, td, FOX_HEADS))
            outs["mla_p"].append(lat32[:n_p].reshape(b, t, MLA_LAT))
            outs["mla_s"].append(lat32[n_p:].reshape(bd, td, MLA_LAT))
        else:
            w, bg, pe2, wlo, whi, w2bd = _prep_odd(od_w_in[e], od_b_gate[e], nsa_cmp_pe[e], nsa_cmp_w1[e],
                                                   nsa_cmp_w2[e])
            q, cmp32, slc32, wn32, slc16, wn16, gates = _odd_proj(
                h, row(norm_w[li, 1]), w, bg, rope_nsa, tm, t, n_p)
            kvc = _compress_prompt(cmp32[:n_p].reshape(n_p // CMP_D, CHUNK_W), pe2, wlo, whi, w2bd,
                                   b, t // CMP_D)
            o_p = _nsa_prompt(q, gates, kvc, slc16, wn16, mimp_p, eye_q, expand_p, b, t, tq, tk)
            ocmp, mask = _nsa_dec1(page_table, q, pe2, wlo, whi, w2bd, mimp_s, expand_s, cache_cmp2,
                                   e, n_p, bd, td, ns_pad)
            o_s, nst = _nsa_dec2(page_table, q, gates, ocmp, mask, slc32, wn32, state2, cache_slc2,
                                 e, n_p, bd, td)
            h = _out_proj(h, o_p, o_s, od_w_out[e].astype(BF16), tm)
            wlen = min(WINDOW, t)
            outs["cmp_p"].append(cmp32[:n_p].reshape((b, t) + kv_shape))
            outs["cmp_s"].append(cmp32[n_p:].reshape((bd, td) + kv_shape))
            outs["slc_p"].append(slc32[:n_p].reshape((b, t) + kv_shape))
            outs["slc_s"].append(slc32[n_p:].reshape((bd, td) + kv_shape))
            outs["win_p"].append(wn32[:n_p].reshape((b, t) + kv_shape)[:, t - wlen:])
            outs["win_s"].append(jnp.transpose(nst.reshape((bd,) + kv_shape + (wbuf,)), (0, 4, 1, 2, 3)))
        h = _ffn(h, row(norm_w[li, 2]), wg16[li, 1], wu16[li, 1], wd16[li, 1], tm_ffn, tf)
    y = _final_norm(h, row(final_norm), tm)
    st = lambda k: jnp.stack(outs[k])
    return (y[:n_p].reshape(b, t, D_MODEL), y[n_p:].reshape(bd, td, D_MODEL),
            st("fkv_p"), st("fkv_s"), st("flf_p"), st("flf_s"), st("mla_p"), st("mla_s"),
            st("cmp_p"), st("cmp_s"), st("slc_p"), st("slc_s"), st("win_p"), st("win_s"))
```

```python
import functools
import math

import numpy as np
import jax
import jax.numpy as jnp
from jax import lax
from jax.experimental import pallas as pl
from jax.experimental.pallas import tpu as pltpu

F32 = jnp.float32
BF16 = jnp.bfloat16

D_MODEL = 1024
HEAD_DIM = 64
FOX_HEADS = 8
FOX_KV_HEADS = 2
MLA_HEADS = 8
MLA_Q_LORA = 256
MLA_KV_LORA = 128
MLA_NOPE = 64
MLA_ROPE = 32
MLA_V = 64
MLA_THETA = 10000.0
NSA_HEADS = 16
NSA_KV_HEADS = 2
CMP_L = 32
CMP_D = 16
SEL_L = 64
TOP_N = 16
WINDOW = 512
ROPE_THETA = 500000.0
ROPE_DIM = HEAD_DIM // 4
FF_DIM = 2816
EPS = 1e-6
FORCE_SCORE = 1e9
NEG_INF = -1e30
PAGE = 128

LANES = 128
MLA_LAT = MLA_KV_LORA + MLA_ROPE
MLA_QK = MLA_NOPE + MLA_ROPE
LOG2E = math.log2(math.e)
FOX_SCALE = HEAD_DIM ** -0.5 * LOG2E
MLA_SCALE = MLA_QK ** -0.5 * LOG2E
KV_W = 2 * FOX_KV_HEADS * HEAD_DIM
CHUNK_W = CMP_D * KV_W
VMEM_LIMIT = 56 * 1024 * 1024


def _cp(sem):
    return pltpu.CompilerParams(dimension_semantics=sem, vmem_limit_bytes=VMEM_LIMIT)


def _nt(a, b, precision=None):
    return lax.dot_general(a, b, (((1,), (1,)), ((), ())), precision=precision,
                           preferred_element_type=F32)


def _mm(a, b, precision=None):
    return jnp.dot(a, b, precision=precision, preferred_element_type=F32)


def _rms_val(x, g):
    return (x * lax.rsqrt(jnp.mean(x * x, axis=-1, keepdims=True) + EPS)) * g


def _log_sigmoid(x):
    return -(jnp.maximum(-x, 0.0) + jnp.log1p(jnp.exp(-jnp.abs(x))))


def _gelu_tanh(x):
    return x * (0.5 * (1.0 + jnp.tanh(math.sqrt(2.0 / math.pi) * (x + 0.044715 * (x * x * x)))))


def _iota(shape, dim):
    return lax.broadcasted_iota(jnp.int32, shape, dim)


def _ffn_kernel(x_ref, g_ref, wg_ref, wu_ref, wd_ref, o_ref, xn_ref, acc_ref):
    j = pl.program_id(1)

    @pl.when(j == 0)
    def _():
        xn_ref[...] = _rms_val(x_ref[...], g_ref[...]).astype(BF16)
        acc_ref[...] = jnp.zeros_like(acc_ref)

    xn = xn_ref[...]
    g = _mm(xn, wg_ref[...])
    u = _mm(xn, wu_ref[...])
    a = (g * jax.nn.sigmoid(g)) * u
    acc_ref[...] += _mm(a.astype(BF16), wd_ref[...])

    @pl.when(j == pl.num_programs(1) - 1)
    def _():
        o_ref[...] = x_ref[...] + 0.5 * acc_ref[...]


def _ffn(h, g, wg, wu, wd, tm, tf):
    n = h.shape[0]
    return pl.pallas_call(
        _ffn_kernel,
        grid=(n // tm, FF_DIM // tf),
        in_specs=[
            pl.BlockSpec((tm, D_MODEL), lambda i, j: (i, 0)),
            pl.BlockSpec((1, D_MODEL), lambda i, j: (0, 0)),
            pl.BlockSpec((D_MODEL, tf), lambda i, j: (0, j)),
            pl.BlockSpec((D_MODEL, tf), lambda i, j: (0, j)),
            pl.BlockSpec((tf, D_MODEL), lambda i, j: (j, 0)),
        ],
        out_specs=pl.BlockSpec((tm, D_MODEL), lambda i, j: (i, 0)),
        out_shape=jax.ShapeDtypeStruct((n, D_MODEL), F32),
        scratch_shapes=[pltpu.VMEM((tm, D_MODEL), BF16), pltpu.VMEM((tm, D_MODEL), F32)],
        compiler_params=_cp(("parallel", "arbitrary")),
        name="ffn",
    )(h, g, wg, wu, wd)


def _out_proj_kernel(h_ref, op_ref, os_ref, w_ref, o_ref, *, npt):
    i = pl.program_id(0)

    @pl.when(i < npt)
    def _():
        o_ref[...] = h_ref[...] + _mm(op_ref[...], w_ref[...])

    @pl.when(i >= npt)
    def _():
        o_ref[...] = h_ref[...] + _mm(os_ref[...].astype(BF16), w_ref[...])


def _out_proj(h, o_p, o_s, w, tm):
    n = h.shape[0]
    npt = o_p.shape[0] // tm
    return pl.pallas_call(
        functools.partial(_out_proj_kernel, npt=npt),
        grid=(n // tm,),
        in_specs=[
            pl.BlockSpec((tm, D_MODEL), lambda i: (i, 0)),
            pl.BlockSpec((tm, D_MODEL), lambda i: (jnp.minimum(i, npt - 1), 0)),
            pl.BlockSpec((tm, D_MODEL), lambda i: (jnp.maximum(i - npt, 0), 0)),
            pl.BlockSpec((D_MODEL, D_MODEL), lambda i: (0, 0)),
        ],
        out_specs=pl.BlockSpec((tm, D_MODEL), lambda i: (i, 0)),
        out_shape=jax.ShapeDtypeStruct((n, D_MODEL), F32),
        compiler_params=_cp(("parallel",)),
        name="out_proj",
    )(h, o_p, o_s, w)


def _final_norm_kernel(x_ref, g_ref, o_ref):
    o_ref[...] = _rms_val(x_ref[...], g_ref[...])


def _final_norm(h, g, tm, tile0, n):
    return pl.pallas_call(
        _final_norm_kernel,
        grid=(n // tm,),
        in_specs=[pl.BlockSpec((tm, D_MODEL), lambda i: (tile0 + i, 0)),
                  pl.BlockSpec((1, D_MODEL), lambda i: (0, 0))],
        out_specs=pl.BlockSpec((tm, D_MODEL), lambda i: (i, 0)),
        out_shape=jax.ShapeDtypeStruct((n, D_MODEL), F32),
        compiler_params=_cp(("parallel",)),
        name="final_norm",
    )(h, g)


def _rope_tables(t_prompt, past_len, t_dec, tm, theta, rot_dim, period):
    half = rot_dim // 2
    pos = np.concatenate([np.arange(t_prompt), past_len + (np.arange(tm) % t_dec)]).astype(np.float64)
    inv = np.float64(theta) ** (-np.arange(half, dtype=np.float64) / half)
    ang = pos[:, None] * inv[None, :]
    cos, sin = np.cos(ang), np.sin(ang)
    lane = np.arange(LANES) % period
    c = np.ones((pos.shape[0], LANES))
    s1 = np.zeros((pos.shape[0], LANES))
    s2 = np.zeros((pos.shape[0], LANES))
    lo = lane < half
    hi = (lane >= half) & (lane < rot_dim)
    c[:, lo] = cos[:, lane[lo]]
    c[:, hi] = cos[:, lane[hi] - half]
    s1[:, hi] = sin[:, lane[hi] - half]
    s2[:, lo] = -sin[:, lane[lo]]
    return jnp.asarray(np.stack([c, s1, s2]), dtype=F32)


def _rope_chunk(x, rope_ref, half):
    return (x * rope_ref[0] + pltpu.roll(x, half, 1) * rope_ref[1]
            + pltpu.roll(x, LANES - half, 1) * rope_ref[2])


def _rope_spec(tm, t_prompt, sample):
    tiles = t_prompt // tm
    if sample:
        return pl.BlockSpec((3, tm, LANES), lambda i: (0, tiles, 0))
    return pl.BlockSpec((3, tm, LANES), lambda i: (0, i % tiles, 0))


EV_COLS = 1280
EV_FLG_LANE = 32


def _even_proj_kernel(x_ref, g_ref, win_ref, bf_ref, qn_ref, wqu_ref, wuk_ref, kvn_ref, rope_ref,
                      qf_ref, kv32_ref, kv16_ref, logf_ref, qm_ref, lat32_ref, lat16_ref):
    xn = _rms_val(x_ref[...], g_ref[...]).astype(BF16)
    z = _mm(xn, win_ref[...])
    for h in range(FOX_HEADS):
        qf_ref[h] = z[:, HEAD_DIM * h:HEAD_DIM * (h + 1)] * FOX_SCALE
    kv = z[:, 512:768]
    kv32_ref[...] = kv
    for j in range(4):
        kv16_ref[j] = kv[:, HEAD_DIM * j:HEAD_DIM * (j + 1)].astype(BF16)
    cq = _rms_val(z[:, 768:1024], qn_ref[...]).astype(BF16)
    qmm = _mm(cq, wqu_ref[...])
    qlat = _mm(qmm[:, :512].astype(BF16), wuk_ref[...]) * MLA_SCALE
    for h in range(MLA_HEADS):
        qm_ref[h, :, 0:MLA_KV_LORA] = qlat[:, MLA_KV_LORA * h:MLA_KV_LORA * (h + 1)]
    for c in range(2):
        pe = _rope_chunk(qmm[:, 512 + LANES * c:512 + LANES * (c + 1)], rope_ref, MLA_ROPE // 2) * MLA_SCALE
        for hh in range(4):
            qm_ref[4 * c + hh, :, MLA_KV_LORA:MLA_LAT] = pe[:, MLA_ROPE * hh:MLA_ROPE * (hh + 1)]
    ckv = _rms_val(z[:, 1024:1152], kvn_ref[...])
    last = z[:, 1152:1280]
    kpe = _rope_chunk(last, rope_ref, MLA_ROPE // 2)[:, 0:MLA_ROPE]
    lat32_ref[:, 0:MLA_KV_LORA] = ckv
    lat32_ref[:, MLA_KV_LORA:MLA_LAT] = kpe
    lat16_ref[:, 0:MLA_KV_LORA] = ckv.astype(BF16)
    lat16_ref[:, MLA_KV_LORA:MLA_LAT] = kpe.astype(BF16)
    logf_ref[...] = _log_sigmoid(last + bf_ref[...])[:, EV_FLG_LANE:EV_FLG_LANE + FOX_HEADS]


def _even_proj(h, g, w_in, b_f, q_norm, w_q_up, wuk_bd, kv_norm, rope, tm, t_prompt, tile0, n, sample):
    full = lambda shape: pl.BlockSpec(shape, lambda i: (0,) * len(shape))
    return pl.pallas_call(
        _even_proj_kernel,
        grid=(n // tm,),
        in_specs=[
            pl.BlockSpec((tm, D_MODEL), lambda i: (tile0 + i, 0)),
            full((1, D_MODEL)), full((D_MODEL, EV_COLS)), full((1, LANES)), full((1, MLA_Q_LORA)),
            full((MLA_Q_LORA, 768)), full((512, 1024)), full((1, MLA_KV_LORA)),
            _rope_spec(tm, t_prompt, sample),
        ],
        out_specs=[
            pl.BlockSpec((FOX_HEADS, tm, HEAD_DIM), lambda i: (0, i, 0)),
            pl.BlockSpec((tm, KV_W), lambda i: (i, 0)),
            pl.BlockSpec((4, tm, HEAD_DIM), lambda i: (0, i, 0)),
            pl.BlockSpec((tm, FOX_HEADS), lambda i: (i, 0)),
            pl.BlockSpec((MLA_HEADS, tm, MLA_LAT), lambda i: (0, i, 0)),
            pl.BlockSpec((tm, MLA_LAT), lambda i: (i, 0)),
            pl.BlockSpec((tm, MLA_LAT), lambda i: (i, 0)),
        ],
        out_shape=[
            jax.ShapeDtypeStruct((FOX_HEADS, n, HEAD_DIM), F32),
            jax.ShapeDtypeStruct((n, KV_W), F32),
            jax.ShapeDtypeStruct((4, n, HEAD_DIM), BF16),
            jax.ShapeDtypeStruct((n, FOX_HEADS), F32),
            jax.ShapeDtypeStruct((MLA_HEADS, n, MLA_LAT), F32),
            jax.ShapeDtypeStruct((n, MLA_LAT), F32),
            jax.ShapeDtypeStruct((n, MLA_LAT), BF16),
        ],
        compiler_params=_cp(("parallel",)),
        name="even_proj",
    )(h, g, w_in, b_f, q_norm, w_q_up, wuk_bd, kv_norm, rope)


CS_CHUNK = 128


def _csum_kernel(lf_ref, cs_ref, cst_ref, *, t, tk):
    r = _iota((CS_CHUNK, CS_CHUNK), 0)
    c = _iota((CS_CHUNK, CS_CHUNK), 1)
    tri = (c <= r).astype(F32)
    eye = (_iota((FOX_HEADS, FOX_HEADS), 0) == _iota((FOX_HEADS, FOX_HEADS), 1)).astype(F32)
    carry = jnp.zeros((1, FOX_HEADS), F32)
    per = tk // CS_CHUNK
    for k in range(t // CS_CHUNK):
        lf = lf_ref[k * CS_CHUNK:(k + 1) * CS_CHUNK, :]
        cs = _mm(tri, lf, lax.Precision.HIGHEST) + carry
        carry = cs[CS_CHUNK - 1:CS_CHUNK, :]
        cs_ref[k * CS_CHUNK:(k + 1) * CS_CHUNK, :] = cs
        cst_ref[k // per, :, (k % per) * CS_CHUNK:(k % per + 1) * CS_CHUNK] = _nt(eye, cs, lax.Precision.HIGHEST)


def _csum(logf, b, t, tk):
    return pl.pallas_call(
        functools.partial(_csum_kernel, t=t, tk=tk),
        grid=(b,),
        in_specs=[pl.BlockSpec((t, FOX_HEADS), lambda i: (i, 0))],
        out_specs=[pl.BlockSpec((t, FOX_HEADS), lambda i: (i, 0)),
                   pl.BlockSpec((None, t // tk, FOX_HEADS, tk), lambda i: (i, 0, 0, 0))],
        out_shape=[jax.ShapeDtypeStruct((b * t, FOX_HEADS), F32),
                   jax.ShapeDtypeStruct((b, t // tk, FOX_HEADS, tk), F32)],
        compiler_params=_cp(("parallel",)),
        name="fox_csum",
    )(logf)


def _lane_chunks(s):
    return [s[:, LANES * c:LANES * (c + 1)] for c in range(s.shape[1] // LANES)]


def _online_update(chunks, m_ref, l_ref, acc_ref, v, idx):
    m_prev = m_ref[idx]
    mx = chunks[0]
    for c in chunks[1:]:
        mx = jnp.maximum(mx, c)
    m_new = jnp.maximum(m_prev, jnp.broadcast_to(jnp.max(mx, axis=-1, keepdims=True), m_prev.shape))
    alpha = jnp.exp2(m_prev - m_new)
    ps = [jnp.exp2(c - m_new) for c in chunks]
    lsum = ps[0]
    for p in ps[1:]:
        lsum = lsum + p
    l_ref[idx] = alpha * l_ref[idx] + lsum
    dv = acc_ref.shape[-1]
    acc_ref[idx] = alpha[:, :dv] * acc_ref[idx] + _mm(jnp.concatenate(ps, axis=1).astype(BF16), v)
    m_ref[idx] = m_new


def _online_finish(l_ref, acc_ref, idx):
    return acc_ref[idx] / jnp.sum(l_ref[idx], axis=-1, keepdims=True)


def _even_attn_kernel(qf_ref, qm_ref, cs_ref, kv_ref, lat_ref, cst_ref, wuv_ref, o_ref,
                      mf_ref, lf_ref, af_ref, mm_ref, lm_ref, am_ref, *, tq, tk):
    qi = pl.program_id(1)
    qs = qi * tq
    rq = FOX_HEADS // FOX_KV_HEADS
    nch = tk // LANES
    mf_ref[...] = jnp.full_like(mf_ref, NEG_INF)
    lf_ref[...] = jnp.zeros_like(lf_ref)
    af_ref[...] = jnp.zeros_like(af_ref)
    mm_ref[...] = jnp.full_like(mm_ref, NEG_INF)
    lm_ref[...] = jnp.zeros_like(lm_ref)
    am_ref[...] = jnp.zeros_like(am_ref)
    cs = cs_ref[...] * LOG2E
    cqb = [jnp.broadcast_to(cs[:, h:h + 1], (tq, LANES)) for h in range(FOX_HEADS)]
    qg =[qf_ref[rq * g:rq * (g + 1)].reshape(rq * tq, HEAD_DIM).astype(BF16) for g in range(FOX_KV_HEADS)]
    qm = qm_ref[...].reshape(MLA_HEADS * tq, MLA_LAT).astype(BF16)
    qpos = qs + _iota((tq, LANES), 0)

    def step(j, masked):
        ks = pl.multiple_of(j * tk, tk)
        if masked:
            keep = [(ks + LANES * c + _iota((tq, LANES), 1)) <= qpos for c in range(nch)]
        ck = cst_ref[j] * LOG2E
        for g in range(FOX_KV_HEADS):
            k = kv_ref[g, pl.ds(ks, tk), :]
            v = kv_ref[FOX_KV_HEADS + g, pl.ds(ks, tk), :]
            s = _nt(qg[g], k)
            chunks = []
            for c in range(nch):
                parts = []
                for r in range(rq):
                    h = rq * g + r
                    sr = s[r * tq:(r + 1) * tq, LANES * c:LANES * (c + 1)] + (
                        cqb[h] - ck[h:h + 1, LANES * c:LANES * (c + 1)])
                    parts.append(jnp.where(keep[c], sr, NEG_INF) if masked else sr)
                chunks.append(jnp.concatenate(parts, axis=0))
            _online_update(chunks, mf_ref, lf_ref, af_ref, v, g)
        lat = lat_ref[pl.ds(ks, tk), :]
        chunks = _lane_chunks(_nt(qm, lat))
        if masked:
            chunks = [jnp.where(keep[c][None], chunks[c].reshape(MLA_HEADS, tq, LANES), NEG_INF)
                      .reshape(MLA_HEADS * tq, LANES) for c in range(nch)]
        _online_update(chunks, mm_ref, lm_ref, am_ref, lat[:, :MLA_KV_LORA], 0)

    n_full = qs // tk
    n_all = (qs + tq - 1) // tk + 1

    def full_body(j, c):
        step(j, False)
        return c

    def diag_body(j, c):
        step(j, True)
        return c

    lax.fori_loop(0, n_full, full_body, 0)
    lax.fori_loop(n_full, n_all, diag_body, 0)

    for g in range(FOX_KV_HEADS):
        o = _online_finish(lf_ref, af_ref, g)
        for r in range(rq):
            h = rq * g + r
            o_ref[:, HEAD_DIM * h:HEAD_DIM * (h + 1)] = o[r * tq:(r + 1) * tq].astype(BF16)
    ol = _online_finish(lm_ref, am_ref, 0).astype(BF16)
    base = FOX_HEADS * HEAD_DIM
    for h in range(MLA_HEADS):
        om = _mm(ol[h * tq:(h + 1) * tq], wuv_ref[h])
        o_ref[:, base + MLA_V * h:base + MLA_V * (h + 1)] = om.astype(BF16)


def _even_attn(qf, qm, csum, kv16, lat16, cst, wuv, b, t, tq, tk):
    nq = t // tq
    rq = FOX_HEADS // FOX_KV_HEADS
    return pl.pallas_call(
        functools.partial(_even_attn_kernel, tq=tq, tk=tk),
        grid=(b, nq),
        in_specs=[
            pl.BlockSpec((FOX_HEADS, tq, HEAD_DIM), lambda i, j: (0, i * nq + j, 0)),
            pl.BlockSpec((MLA_HEADS, tq, MLA_LAT), lambda i, j: (0, i * nq + j, 0)),
            pl.BlockSpec((tq, FOX_HEADS), lambda i, j: (i * nq + j, 0)),
            pl.BlockSpec((4, t, HEAD_DIM), lambda i, j: (0, i, 0)),
            pl.BlockSpec((t, MLA_LAT), lambda i, j: (i, 0)),
            pl.BlockSpec((None, t // tk, FOX_HEADS, tk), lambda i, j: (i, 0, 0, 0)),
            pl.BlockSpec((MLA_HEADS, MLA_KV_LORA, MLA_V), lambda i, j: (0, 0, 0)),
        ],
        out_specs=pl.BlockSpec((tq, D_MODEL), lambda i, j: (i * nq + j, 0)),
        out_shape=jax.ShapeDtypeStruct((b * t, D_MODEL), BF16),
        scratch_shapes=[
            pltpu.VMEM((FOX_KV_HEADS, rq * tq, LANES), F32), pltpu.VMEM((FOX_KV_HEADS, rq * tq, LANES), F32),
            pltpu.VMEM((FOX_KV_HEADS, rq * tq, HEAD_DIM), F32),
            pltpu.VMEM((1, MLA_HEADS * tq, LANES), F32), pltpu.VMEM((1, MLA_HEADS * tq, LANES), F32),
            pltpu.VMEM((1, MLA_HEADS * tq, MLA_KV_LORA), F32),
        ],
        compiler_params=_cp(("parallel", "arbitrary")),
        name="even_attn_prompt",
    )(qf, qm, csum, kv16, lat16, cst, wuv)


def _split3(x):
    hi = x.astype(BF16)
    r1 = x - hi.astype(F32)
    mid = r1.astype(BF16)
    return hi, mid, (r1 - mid.astype(F32)).astype(BF16)


def _mm_exact_l(x, w):
    hi, mid, lo = _split3(x)
    return _mm(hi, w) + _mm(mid, w) + _mm(lo, w)


def _mm_exact_r(w, x):
    hi, mid, lo = _split3(x)
    return _mm(w, hi) + _mm(w, mid) + _mm(w, lo)


def _joint_softmax(s_past, s_new, pv_past, pv_new):
    mx = jnp.maximum(jnp.max(s_past, axis=-1, keepdims=True), jnp.max(s_new, axis=-1, keepdims=True))
    p1 = jnp.exp2(s_past - mx)
    p2 = jnp.exp2(s_new - mx)
    den = jnp.sum(p1, axis=-1, keepdims=True) + jnp.sum(p2, axis=-1, keepdims=True)
    return (pv_past(p1.astype(BF16)) + pv_new(p2.astype(BF16))) / den


def _even_dec_kernel(pt_ref, qf_ref, qm_ref, kvn_ref, latn_ref, lfn_ref, wuv_ref, tri_ref, ones_ref, pre_ref,
                     *rest, n_pages, td):
    kv_refs = rest[:n_pages]
    lf_refs = rest[n_pages:2 * n_pages]
    lat_refs = rest[2 * n_pages:3 * n_pages]
    o_ref = rest[3 * n_pages]
    rq = FOX_HEADS // FOX_KV_HEADS
    rows = rq * td

    lf = jnp.concatenate([lf_refs[i][...] for i in range(n_pages)], axis=0)
    cum = _mm_exact_l(lf, tri_ref[...])
    tot = _mm_exact_l(lf, ones_ref[...])
    pre = _mm_exact_r(pre_ref[...], tot)
    ck = (cum + pre) * LOG2E
    last = FOX_HEADS * (n_pages - 1)
    past_total = (pre + tot)[last:last + FOX_HEADS] * LOG2E

    tri_n = (_iota((td, td), 1) <= _iota((td, td), 0)).astype(F32)
    eye = (_iota((FOX_HEADS, FOX_HEADS), 0) == _iota((FOX_HEADS, FOX_HEADS), 1)).astype(F32)
    csn = _mm(tri_n, lfn_ref[...], lax.Precision.HIGHEST) * LOG2E
    ckn = _nt(eye, csn, lax.Precision.HIGHEST) + past_total[:, 0:td]
    cq = [csn[:, h:h + 1] + past_total[h:h + 1, :] for h in range(FOX_HEADS)]
    kvn = kvn_ref[...].astype(BF16)
    keep = _iota((rows, td), 1) <= _iota((rows, td), 0) % td

    def rows_of(refs, lo, hi):
        return jnp.concatenate([refs[i][lo:hi, :].astype(BF16) for i in range(n_pages)], axis=1)

    for g in range(FOX_KV_HEADS):
        q = qf_ref[rq * g:rq * (g + 1)].reshape(rows, HEAD_DIM).astype(BF16)
        kt = rows_of(kv_refs, HEAD_DIM * g, HEAD_DIM * (g + 1))
        vo = (FOX_KV_HEADS + g) * HEAD_DIM
        vt = rows_of(kv_refs, vo, vo + HEAD_DIM)
        s = _mm(q, kt)
        parts = []
        for r in range(rq):
            h = rq * g + r
            bias = jnp.concatenate([cq[h] - ck[FOX_HEADS * i + h:FOX_HEADS * i + h + 1, :]
                                    for i in range(n_pages)], axis=1)
            parts.append(s[r * td:(r + 1) * td] + bias)
        s = jnp.concatenate(parts, axis=0)
        sn = _nt(q, kvn[:, HEAD_DIM * g:HEAD_DIM * (g + 1)])
        sn = jnp.concatenate([sn[r * td:(r + 1) * td] + (cq[rq * g + r][:, 0:td] - ckn[rq * g + r:rq * g + r + 1, :])
                              for r in range(rq)], axis=0)
        sn = jnp.where(keep, sn, NEG_INF)
        o = _joint_softmax(s, sn, lambda p: _nt(p, vt), lambda p: _mm(p, kvn[:, vo:vo + HEAD_DIM]))
        for r in range(rq):
            h = rq * g + r
            o_ref[:, HEAD_DIM * h:HEAD_DIM * (h + 1)] = o[r * td:(r + 1) * td]

    latt = rows_of(lat_refs, 0, MLA_LAT)
    latn = latn_ref[...].astype(BF16)
    qm = qm_ref[...].reshape(MLA_HEADS * td, MLA_LAT).astype(BF16)
    keep_m = _iota((MLA_HEADS * td, td), 1) <= _iota((MLA_HEADS * td, td), 0) % td
    sn = jnp.where(keep_m, _nt(qm, latn), NEG_INF)
    ol = _joint_softmax(_mm(qm, latt), sn, lambda p: _nt(p, latt[:MLA_KV_LORA]),
                        lambda p: _mm(p, latn[:, :MLA_KV_LORA])).astype(BF16)
    base = FOX_HEADS * HEAD_DIM
    for h in range(MLA_HEADS):
        o_ref[:, base + MLA_V * h:base + MLA_V * (h + 1)] = _mm(ol[h * td:(h + 1) * td], wuv_ref[h])


def _even_dec(page_table, qf, qm, kv32, lat32, logf, wuv, cache_kv, cache_lft, cache_lat, e, n_p, bd, td):
    n_pages = page_table.shape[1]
    off = n_p // td
    nr = n_pages * FOX_HEADS
    tri = jnp.asarray(np.triu(np.ones((PAGE, PAGE), np.float32)), dtype=BF16)
    ones = jnp.ones((PAGE, PAGE), BF16)
    r = np.arange(nr)
    pre = ((r[None, :] // FOX_HEADS < r[:, None] // FOX_HEADS) & (r[None, :] % FOX_HEADS == r[:, None] % FOX_HEADS))
    pre = jnp.asarray(pre.astype(np.float32), dtype=BF16)

    def page_spec(shape, i):
        return pl.BlockSpec((None, None) + shape, lambda b, pt: (e, pt[b, i], 0, 0))

    full = lambda shape: pl.BlockSpec(shape, lambda b, pt: (0,) * len(shape))
    in_specs = [
        pl.BlockSpec((FOX_HEADS, td, HEAD_DIM), lambda b, pt: (0, off + b, 0)),
        pl.BlockSpec((MLA_HEADS, td, MLA_LAT), lambda b, pt: (0, off + b, 0)),
        pl.BlockSpec((td, KV_W), lambda b, pt: (off + b, 0)),
        pl.BlockSpec((td, MLA_LAT), lambda b, pt: (off + b, 0)),
        pl.BlockSpec((td, FOX_HEADS), lambda b, pt: (off + b, 0)),
        full((MLA_HEADS, MLA_KV_LORA, MLA_V)), full((PAGE, PAGE)), full((PAGE, PAGE)), full((nr, nr)),
    ]
    in_specs += [page_spec((KV_W, PAGE), i) for i in range(n_pages)]
    in_specs += [page_spec((FOX_HEADS, PAGE), i) for i in range(n_pages)]
    in_specs += [page_spec((MLA_LAT, PAGE), i) for i in range(n_pages)]
    grid_spec = pltpu.PrefetchScalarGridSpec(
        num_scalar_prefetch=1,
        grid=(bd,),
        in_specs=in_specs,
        out_specs=pl.BlockSpec((td, D_MODEL), lambda b, pt: (b, 0)),
    )
    return pl.pallas_call(
        functools.partial(_even_dec_kernel, n_pages=n_pages, td=td),
        grid_spec=grid_spec,
        out_shape=jax.ShapeDtypeStruct((bd * td, D_MODEL), F32),
        compiler_params=_cp(("parallel",)),
        name="even_attn_decode",
    )(page_table, qf, qm, kv32, lat32, logf, wuv, tri, ones, pre,
      *([cache_kv] * n_pages), *([cache_lft] * n_pages), *([cache_lat] * n_pages))


OD_COLS = 1920
N_GATES = 3 * NSA_HEADS


def _odd_proj_kernel(x_ref, g_ref, win_ref, bg_ref, rope_ref,
                     q_ref, cmp_ref, slc_ref, wn_ref, slc16_ref, wn16_ref, gate_ref):
    xn = _rms_val(x_ref[...], g_ref[...]).astype(BF16)
    z = _mm(xn, win_ref[...])
    half = ROPE_DIM // 2
    for c in range(NSA_HEADS // 2):
        qc = _rope_chunk(z[:, LANES * c:LANES * (c + 1)], rope_ref, half) * FOX_SCALE
        q_ref[2 * c] = qc[:, :HEAD_DIM]
        q_ref[2 * c + 1] = qc[:, HEAD_DIM:]
    ko, vo = NSA_HEADS * HEAD_DIM, NSA_HEADS * HEAD_DIM + 3 * LANES
    for n, (r32, r16) in enumerate(((cmp_ref, None), (slc_ref, slc16_ref), (wn_ref, wn16_ref))):
        k = _rope_chunk(z[:, ko + LANES * n:ko + LANES * (n + 1)], rope_ref, half)
        v = z[:, vo + LANES * n:vo + LANES * (n + 1)]
        r32[:, 0:LANES] = k
        r32[:, LANES:2 * LANES] = v
        if r16 is not None:
            for g in range(NSA_KV_HEADS):
                r16[g] = k[:, HEAD_DIM * g:HEAD_DIM * (g + 1)].astype(BF16)
                r16[NSA_KV_HEADS + g] = v[:, HEAD_DIM * g:HEAD_DIM * (g + 1)].astype(BF16)
    gate_ref[...] = jax.nn.sigmoid(z[:, vo + 3 * LANES:vo + 4 * LANES] + bg_ref[...])[:, 0:N_GATES]


def _odd_proj(h, g, w_in, b_gate, rope, tm, t_prompt, tile0, n, sample):
    full = lambda shape: pl.BlockSpec(shape, lambda i: (0,) * len(shape))
    row = lambda w: pl.BlockSpec((tm, w), lambda i: (i, 0))
    hm = lambda k: pl.BlockSpec((k, tm, HEAD_DIM), lambda i: (0, i, 0))
    return pl.pallas_call(
        _odd_proj_kernel,
        grid=(n // tm,),
        in_specs=[pl.BlockSpec((tm, D_MODEL), lambda i: (tile0 + i, 0)), full((1, D_MODEL)),
                  full((D_MODEL, OD_COLS)), full((1, LANES)), _rope_spec(tm, t_prompt, sample)],
        out_specs=[hm(NSA_HEADS), row(KV_W), row(KV_W), row(KV_W), hm(4), hm(4), row(N_GATES)],
        out_shape=[
            jax.ShapeDtypeStruct((NSA_HEADS, n, HEAD_DIM), F32),
            jax.ShapeDtypeStruct((n, KV_W), F32), jax.ShapeDtypeStruct((n, KV_W), F32),
            jax.ShapeDtypeStruct((n, KV_W), F32),
            jax.ShapeDtypeStruct((4, n, HEAD_DIM), BF16), jax.ShapeDtypeStruct((4, n, HEAD_DIM), BF16),
            jax.ShapeDtypeStruct((n, N_GATES), F32),
        ],
        compiler_params=_cp(("parallel",)),
        name="odd_proj",
    )(h, g, w_in, b_gate, rope)


def _compress_rows(ch, pe_ref, wlo_ref, whi_ref, w2_ref):
    a = _mm((ch + pe_ref[0:1, :]).astype(BF16), wlo_ref[...])
    b = _mm((ch + pe_ref[1:2, :]).astype(BF16), whi_ref[...])
    hid = a + pltpu.roll(b, ch.shape[0] - 1, 0)
    return _mm(_gelu_tanh(hid).astype(BF16), w2_ref[...])


def _compress_prompt_kernel(ch_ref, pe_ref, wlo_ref, whi_ref, w2_ref, o_ref):
    o_ref[...] = _compress_rows(ch_ref[...], pe_ref, wlo_ref, whi_ref, w2_ref).astype(BF16)


def _compress_prompt(chunks, pe, wlo, whi, w2, b, rows):
    full = lambda shape: pl.BlockSpec(shape, lambda i: (0,) * len(shape))
    return pl.pallas_call(
        _compress_prompt_kernel,
        grid=(b,),
        in_specs=[pl.BlockSpec((rows, CHUNK_W), lambda i: (i, 0)),
                  full((2, CHUNK_W)), full((CHUNK_W, KV_W)), full((CHUNK_W, KV_W)), full((KV_W, KV_W))],
        out_specs=pl.BlockSpec((rows, KV_W), lambda i: (i, 0)),
        out_shape=jax.ShapeDtypeStruct((b * rows, KV_W), BF16),
        compiler_params=_cp(("parallel",)),
        name="nsa_compress_prompt",
    )(chunks, pe, wlo, whi, w2)


def _cmp_branch(q, kc, vc, qpos_col, heads, t):
    ncp = kc.shape[0]
    s = _nt(q, kc).reshape(heads, t, ncp)
    mc = (_iota((t, ncp), 1) * CMP_D + (CMP_L - 1)) <= qpos_col
    s = jnp.where(mc[None], s, NEG_INF)
    e = jnp.exp2(s - jnp.max(s, axis=-1, keepdims=True))
    p = jnp.where(mc[None], e / jnp.sum(e, axis=-1, keepdims=True), 0.0)
    o = _mm(p.reshape(heads * t, ncp).astype(BF16), vc)
    return o, jnp.sum(p, axis=0)


def _select_blocks(imp_t, qpos_row, sc_ref, n_real):
    ns, c = imp_t.shape
    blk = _iota((ns, c), 0)
    cur = qpos_row // SEL_L
    forced = (blk == 0) | (blk == cur) | (blk == cur - 1)
    valid = (blk * SEL_L <= qpos_row) & (blk < n_real)
    score = jnp.where(valid, jnp.where(forced, FORCE_SCORE, imp_t), -1.0)
    score = jnp.where(blk < n_real, score, -2.0)
    sc_ref[...] = score

    def body(i, rank):
        row = sc_ref[pl.ds(i, 1), :]
        ahead = (row > score) | ((row == score) & (i < blk))
        return rank + ahead.astype(F32)

    rank = lax.fori_loop(0, n_real, body, jnp.zeros((ns, c), F32))
    return (rank < float(min(TOP_N, n_real))).astype(F32)


def _select_blocks_rows(imp, qpos_col, n_real):
    c, ns = imp.shape
    blk = _iota((c, ns), 1)
    cur = qpos_col // SEL_L
    forced = (blk == 0) | (blk == cur) | (blk == cur - 1)
    valid = (blk * SEL_L <= qpos_col) & (blk < n_real)
    score = jnp.where(valid, jnp.where(forced, FORCE_SCORE, imp), -1.0)
    score = jnp.where(blk < n_real, score, -2.0)
    rank = jnp.zeros((c, ns), F32)
    for i in range(n_real):
        col = score[:, i:i + 1]
        ahead = (col > score) | ((col == score) & (blk > i))
        rank = rank + ahead.astype(F32)
    return (rank < float(min(TOP_N, n_real))).astype(F32)


def _imp_matrix(ns_pad, ncp):
    ratio, span = SEL_L // CMP_D, CMP_L // CMP_D
    m = np.zeros((ns_pad, ncp), np.float32)
    for j in range(ns_pad):
        for a in range(ratio):
            for b in range(span):
                cc = j * ratio + a + b
                if cc < ncp:
                    m[j, cc] += 1.0
    return jnp.asarray(m)


def _expand_matrix(ns_pad, n_keys):
    m = (np.arange(n_keys)[None, :] // SEL_L == np.arange(ns_pad)[:, None]).astype(np.float32)
    return jnp.asarray(m, dtype=BF16)


def _nsa_prompt_kernel(q_ref, gate_ref, kvc_ref, slc_ref, wn_ref, mimp_ref, eye_ref, exp_ref, o_ref,
                       m_ref, l_ref, acc_ref, mask_ref, sc_ref, *, tq, tk, t):
    qi = pl.program_id(1)
    qs = qi * tq
    rq = NSA_HEADS // NSA_KV_HEADS
    ns = t // SEL_L
    qpos_col = qs + _iota((tq, 1), 0)
    qpos_row = qs + _iota((1, tq), 1)
    kvc = kvc_ref[...]
    gates = gate_ref[...]
    nch = tk // LANES
    causal = (_iota((tq, t), 1) <= qs + _iota((tq, t), 0)).astype(F32)
    wspan = min(WINDOW + tq, t)
    w0 = pl.multiple_of(jnp.minimum(jnp.maximum(qs - WINDOW, 0), t - wspan), tq)
    dwin = (qs + _iota((tq, wspan), 0)) - (w0 + _iota((tq, wspan), 1))
    keep_win = (dwin >= 0) & (dwin < WINDOW)

    def masked_chunks(s, keep):
        out = []
        for c in range(s.shape[1] // LANES):
            sl = slice(LANES * c, LANES * (c + 1))
            out.append(jnp.where(keep[:, sl][None], s[:, sl].reshape(rq, tq, LANES), NEG_INF)
                       .reshape(rq * tq, LANES))
        return out

    for g in range(NSA_KV_HEADS):
        q = q_ref[rq * g:rq * (g + 1)].reshape(rq * tq, HEAD_DIM).astype(BF16)
        kc = kvc[:, HEAD_DIM * g:HEAD_DIM * (g + 1)]
        vc = kvc[:, HEAD_DIM * (NSA_KV_HEADS + g):HEAD_DIM * (NSA_KV_HEADS + g + 1)]
        o_cmp, pgrp = _cmp_branch(q, kc, vc, qpos_col, rq, tq)
        imp_t = _nt(mimp_ref[...], pgrp, lax.Precision.HIGHEST)
        sel_t = _select_blocks(imp_t, qpos_row, sc_ref, ns)
        sel = _nt(eye_ref[...], sel_t.astype(BF16))
        mfull = _mm(sel.astype(BF16), exp_ref[...]) * causal
        for jj in range(t // tk):
            mask_ref[jj] = mfull[:, jj * tk:(jj + 1) * tk]

        m_ref[...] = jnp.full_like(m_ref, NEG_INF)
        l_ref[...] = jnp.zeros_like(l_ref)
        acc_ref[...] = jnp.zeros_like(acc_ref)

        def slc_body(j, c):
            ks = pl.multiple_of(j * tk, tk)
            k = slc_ref[g, pl.ds(ks, tk), :]
            v = slc_ref[NSA_KV_HEADS + g, pl.ds(ks, tk), :]
            _online_update(masked_chunks(_nt(q, k), mask_ref[j] > 0.5), m_ref, l_ref, acc_ref, v, 0)
            return c

        n_all = (qs + tq - 1) // tk + 1
        lax.fori_loop(0, n_all, slc_body, 0)
        o_slc = _online_finish(l_ref, acc_ref, 0)

        kw = wn_ref[g, pl.ds(w0, wspan), :]
        vw = wn_ref[NSA_KV_HEADS + g, pl.ds(w0, wspan), :]
        chunks = masked_chunks(_nt(q, kw), keep_win)
        mx = chunks[0]
        for c in chunks[1:]:
            mx = jnp.maximum(mx, c)
        mx = jnp.broadcast_to(jnp.max(mx, axis=-1, keepdims=True), mx.shape)
        ps = [jnp.exp2(c - mx) for c in chunks]
        den = ps[0]
        for p in ps[1:]:
            den = den + p
        o_win = _mm(jnp.concatenate(ps, axis=1).astype(BF16), vw) / jnp.sum(den, axis=-1, keepdims=True)

        for r in range(rq):
            h = rq * g + r
            rows = slice(r * tq, (r + 1) * tq)
            o = (gates[:, h:h + 1] * o_cmp[rows] + gates[:, NSA_HEADS + h:NSA_HEADS + h + 1] * o_slc[rows]
                 + gates[:, 2 * NSA_HEADS + h:2 * NSA_HEADS + h + 1] * o_win[rows])
            o_ref[:, HEAD_DIM * h:HEAD_DIM * (h + 1)] = o.astype(BF16)


def _nsa_prompt(q, gates, kvc, slc16, wn16, mimp, eye, expand, b, t, tq, tk):
    nq = t // tq
    rq = NSA_HEADS // NSA_KV_HEADS
    ncp = t // CMP_D
    ns = t // SEL_L
    return pl.pallas_call(
        functools.partial(_nsa_prompt_kernel, tq=tq, tk=tk, t=t),
        grid=(b, nq),
        in_specs=[
            pl.BlockSpec((NSA_HEADS, tq, HEAD_DIM), lambda i, j: (0, i * nq + j, 0)),
            pl.BlockSpec((tq, N_GATES), lambda i, j: (i * nq + j, 0)),
            pl.BlockSpec((ncp, KV_W), lambda i, j: (i, 0)),
            pl.BlockSpec((4, t, HEAD_DIM), lambda i, j: (0, i, 0)),
            pl.BlockSpec((4, t, HEAD_DIM), lambda i, j: (0, i, 0)),
            pl.BlockSpec((ns, ncp), lambda i, j: (0, 0)),
            pl.BlockSpec((tq, tq), lambda i, j: (0, 0)),
            pl.BlockSpec((ns, t), lambda i, j: (0, 0)),
        ],
        out_specs=pl.BlockSpec((tq, D_MODEL), lambda i, j: (i * nq + j, 0)),
        out_shape=jax.ShapeDtypeStruct((b * t, D_MODEL), BF16),
        scratch_shapes=[
            pltpu.VMEM((1, rq * tq, LANES), F32), pltpu.VMEM((1, rq * tq, LANES), F32),
            pltpu.VMEM((1, rq * tq, HEAD_DIM), F32),
            pltpu.VMEM((t // tk, tq, tk), F32),
            pltpu.VMEM((ns, tq), F32),
        ],
        compiler_params=_cp(("parallel", "arbitrary")),
        name="nsa_attn_prompt",
    )(q, gates, kvc, slc16, wn16, mimp, eye, expand)


def _nsa_dec1_kernel(pt_ref, q_ref, pe_ref, wlo_ref, whi_ref, w2_ref, mimp_ref, exp_ref, *rest,
                     n_pages, td, past_len, ns_pad):
    pages = rest[:n_pages]
    ocmp_ref, mask_ref, xs_ref = rest[n_pages:]
    rq = NSA_HEADS // NSA_KV_HEADS
    n_real = -(-(past_len + td) // SEL_L)
    nchunk = past_len // CMP_D
    for i in range(n_pages):
        x = pages[i][...].T
        xs_ref[0, PAGE * i:PAGE * (i + 1), :] = x[:, :LANES]
        xs_ref[1, PAGE * i:PAGE * (i + 1), :] = x[:, LANES:]
    a = jnp.zeros((nchunk, KV_W), F32)
    b = jnp.zeros((nchunk, KV_W), F32)
    for l in range(CMP_D):
        r = jnp.concatenate([xs_ref[0, pl.ds(l, nchunk, stride=CMP_D), :],
                             xs_ref[1, pl.ds(l, nchunk, stride=CMP_D), :]], axis=1)
        ws = slice(KV_W * l, KV_W * (l + 1))
        a = a + _mm((r + pe_ref[0:1, ws]).astype(BF16), wlo_ref[ws, :])
        b = b + _mm((r + pe_ref[1:2, ws]).astype(BF16), whi_ref[ws, :])
    hid = a + pltpu.roll(b, nchunk - 1, 0)
    kvc = _mm(_gelu_tanh(hid).astype(BF16), w2_ref[...]).astype(BF16)
    qpos_col = past_len + _iota((td, 1), 0)
    pgs = []
    for g in range(NSA_KV_HEADS):
        q = q_ref[rq * g:rq * (g + 1)].reshape(rq * td, HEAD_DIM).astype(BF16)
        kc = kvc[:, HEAD_DIM * g:HEAD_DIM * (g + 1)]
        vc = kvc[:, HEAD_DIM * (NSA_KV_HEADS + g):HEAD_DIM * (NSA_KV_HEADS + g + 1)]
        o_cmp, pgrp = _cmp_branch(q, kc, vc, qpos_col, rq, td)
        for r in range(rq):
            h = rq * g + r
            ocmp_ref[:, HEAD_DIM * h:HEAD_DIM * (h + 1)] = o_cmp[r * td:(r + 1) * td]
        pgs.append(pgrp)
    pg_all = jnp.concatenate(pgs, axis=0)
    imp = _mm(pg_all, mimp_ref[...], lax.Precision.HIGHEST)
    qpos_sel = past_len + _iota((NSA_KV_HEADS * td, 1), 0) % td
    sel = _select_blocks_rows(imp, qpos_sel, n_real)
    mask_ref[...] = _mm(sel.astype(BF16), exp_ref[...]).astype(BF16)


def _nsa_dec1(page_table, q, pe, wlo, whi, w2, mimp, expand, cache_chunks, e, n_p, bd, td, ns_pad):
    n_pages = page_table.shape[1]
    past_len = n_pages * PAGE
    off = n_p // td
    ncp = past_len // CMP_D
    full = lambda shape: pl.BlockSpec(shape, lambda b, pt: (0,) * len(shape))
    in_specs = [
        pl.BlockSpec((NSA_HEADS, td, HEAD_DIM), lambda b, pt: (0, off + b, 0)),
        full((2, CHUNK_W)), full((CHUNK_W, KV_W)), full((CHUNK_W, KV_W)), full((KV_W, KV_W)),
        full((ncp, ns_pad)), full((ns_pad, past_len)),
    ]
    in_specs += [pl.BlockSpec((None, None, KV_W, PAGE), functools.partial(
        lambda b, pt, i: (e, pt[b, i], 0, 0), i=i)) for i in range(n_pages)]
    grid_spec = pltpu.PrefetchScalarGridSpec(
        num_scalar_prefetch=1,
        grid=(bd,),
        in_specs=in_specs,
        out_specs=[pl.BlockSpec((td, D_MODEL), lambda b, pt: (b, 0)),
                   pl.BlockSpec((None, NSA_KV_HEADS * td, past_len), lambda b, pt: (b, 0, 0))],
        scratch_shapes=[pltpu.VMEM((2, past_len, LANES), F32)],
    )
    return pl.pallas_call(
        functools.partial(_nsa_dec1_kernel, n_pages=n_pages, td=td, past_len=past_len, ns_pad=ns_pad),
        grid_spec=grid_spec,
        out_shape=[jax.ShapeDtypeStruct((bd * td, D_MODEL), F32),
                   jax.ShapeDtypeStruct((bd, NSA_KV_HEADS * td, past_len), BF16)],
        compiler_params=_cp(("parallel",)),
        name="nsa_decode_select",
    )(page_table, q, pe, wlo, whi, w2, mimp, expand, *([cache_chunks] * n_pages))


def _nsa_dec2_kernel(pt_ref, q_ref, gate_ref, ocmp_ref, mask_ref, slcn_ref, wnn_ref, state_ref, *rest,
                     n_pages, td, past_len):
    pages = rest[:n_pages]
    o_ref, nst_ref = rest[n_pages:]
    rq = NSA_HEADS // NSA_KV_HEADS
    rows = rq * td
    wbuf = state_ref.shape[1]

    def rows_of(lo, hi):
        return jnp.concatenate([pages[i][lo:hi, :].astype(BF16) for i in range(n_pages)], axis=1)

    qgs = [q_ref[rq * g:rq * (g + 1)].reshape(rows, HEAD_DIM).astype(BF16) for g in range(NSA_KV_HEADS)]
    msk = mask_ref[...].astype(F32)

    gates = gate_ref[...]
    ocmp = ocmp_ref[...]
    slcn = slcn_ref[...].astype(BF16)
    wnn = wnn_ref[...]
    st = state_ref[...]
    eye = (_iota((KV_W, KV_W), 0) == _iota((KV_W, KV_W), 1)).astype(F32)
    place = (_iota((td, wbuf), 1) == _iota((td, wbuf), 0) + (wbuf - td)).astype(F32)
    wnn_t = _nt(eye, wnn, lax.Precision.HIGHEST)
    tail = _mm(wnn_t, place, lax.Precision.HIGHEST)
    nst_ref[...] = jnp.where(_iota((KV_W, wbuf), 1) < wbuf - td, pltpu.roll(st, wbuf - td, 1), tail)
    st16 = st.astype(BF16)
    wnn16 = wnn.astype(BF16)
    keep_new = _iota((rows, td), 1) <= _iota((rows, td), 0) % td
    dpos = (past_len + _iota((rows, wbuf), 0) % td) - (past_len - wbuf + _iota((rows, wbuf), 1))
    keep_st = (dpos >= 0) & (dpos < WINDOW)
    for g in range(NSA_KV_HEADS):
        ko, vo = HEAD_DIM * g, HEAD_DIM * (NSA_KV_HEADS + g)
        kt = rows_of(ko, ko + HEAD_DIM)
        vt = rows_of(vo, vo + HEAD_DIM)
        keep = jnp.broadcast_to((msk[td * g:td * (g + 1)] > 0.5)[None], (rq, td, past_len)).reshape(rows, past_len)
        s = jnp.where(keep, _mm(qgs[g], kt), NEG_INF)
        sn = jnp.where(keep_new, _nt(qgs[g], slcn[:, ko:ko + HEAD_DIM]), NEG_INF)
        o_slc = _joint_softmax(s, sn, lambda p: _nt(p, vt), lambda p: _mm(p, slcn[:, vo:vo + HEAD_DIM]))
        s1 = jnp.where(keep_st, _mm(qgs[g], st16[ko:ko + HEAD_DIM]), NEG_INF)
        s2 = jnp.where(keep_new, _nt(qgs[g], wnn16[:, ko:ko + HEAD_DIM]), NEG_INF)
        o_win = _joint_softmax(s1, s2, lambda p: _nt(p, st16[vo:vo + HEAD_DIM]),
                               lambda p: _mm(p, wnn16[:, vo:vo + HEAD_DIM]))
        for r in range(rq):
            h = rq * g + r
            rs = slice(r * td, (r + 1) * td)
            cols = slice(HEAD_DIM * h, HEAD_DIM * (h + 1))
            o_ref[:, cols] = (gates[:, h:h + 1] * ocmp[:, cols]
                              + gates[:, NSA_HEADS + h:NSA_HEADS + h + 1] * o_slc[rs]
                              + gates[:, 2 * NSA_HEADS + h:2 * NSA_HEADS + h + 1] * o_win[rs])


def _nsa_dec2(page_table, q, gates, ocmp, mask, slc32, wn32, state, cache_slc, e, n_p, bd, td):
    n_pages = page_table.shape[1]
    past_len = n_pages * PAGE
    off = n_p // td
    wbuf = state.shape[3]
    in_specs = [
        pl.BlockSpec((NSA_HEADS, td, HEAD_DIM), lambda b, pt: (0, off + b, 0)),
        pl.BlockSpec((td, N_GATES), lambda b, pt: (off + b, 0)),
        pl.BlockSpec((td, D_MODEL), lambda b, pt: (b, 0)),
        pl.BlockSpec((None, NSA_KV_HEADS * td, past_len), lambda b, pt: (b, 0, 0)),
        pl.BlockSpec((td, KV_W), lambda b, pt: (off + b, 0)),
        pl.BlockSpec((td, KV_W), lambda b, pt: (off + b, 0)),
        pl.BlockSpec((None, None, KV_W, wbuf), lambda b, pt: (e, b, 0, 0)),
    ]
    in_specs += [pl.BlockSpec((None, None, KV_W, PAGE), functools.partial(
        lambda b, pt, i: (e, pt[b, i], 0, 0), i=i)) for i in range(n_pages)]
    grid_spec = pltpu.PrefetchScalarGridSpec(
        num_scalar_prefetch=1,
        grid=(bd,),
        in_specs=in_specs,
        out_specs=[pl.BlockSpec((td, D_MODEL), lambda b, pt: (b, 0)),
                   pl.BlockSpec((None, KV_W, wbuf), lambda b, pt: (b, 0, 0))],
    )
    return pl.pallas_call(
        functools.partial(_nsa_dec2_kernel, n_pages=n_pages, td=td, past_len=past_len),
        grid_spec=grid_spec,
        out_shape=[jax.ShapeDtypeStruct((bd * td, D_MODEL), F32),
                   jax.ShapeDtypeStruct((bd, KV_W, wbuf), F32)],
        compiler_params=_cp(("parallel",)),
        name="nsa_decode_attend",
    )(page_table, q, gates, ocmp, mask, slc32, wn32, state, *([cache_slc] * n_pages))


def _prep_even(w_in, b_f, w_q_up, w_uk, w_uv):
    pad = jnp.zeros((D_MODEL, EV_COLS - 1192), F32)
    w = jnp.concatenate([w_in[:, 0:768], w_in[:, 776:1192], w_in[:, 768:776], pad], axis=1).astype(BF16)
    bf = jnp.zeros((1, LANES), F32).at[0, EV_FLG_LANE:EV_FLG_LANE + FOX_HEADS].set(b_f)
    wq = w_q_up.reshape(MLA_Q_LORA, MLA_HEADS, MLA_QK)
    wq = jnp.concatenate([wq[:, :, :MLA_NOPE].reshape(MLA_Q_LORA, -1),
                          wq[:, :, MLA_NOPE:].reshape(MLA_Q_LORA, -1)], axis=1).astype(BF16)
    wuk_h = jnp.transpose(w_uk, (1, 2, 0))
    eye = jnp.eye(MLA_HEADS, dtype=F32)
    wuk_bd = (eye[:, None, :, None] * wuk_h[:, :, None, :]).reshape(MLA_HEADS * MLA_NOPE, MLA_HEADS * MLA_KV_LORA)
    wuv = jnp.transpose(w_uv, (1, 0, 2)).astype(BF16)
    return w, bf, wq, wuk_bd.astype(BF16), wuv


def _prep_odd(w_in, b_gate, pe, w1, w2):
    qw = NSA_HEADS * HEAD_DIM
    seg = lambda n: w_in[:, qw + LANES * n:qw + LANES * (n + 1)]
    gcol = w_in[:, qw + 6 * LANES:qw + 6 * LANES + N_GATES]
    perm = np.array([h * 3 + c for c in range(3) for h in range(NSA_HEADS)])
    pad = jnp.zeros((D_MODEL, LANES - N_GATES), F32)
    w = jnp.concatenate([w_in[:, :qw], seg(0), seg(2), seg(4), seg(1), seg(3), seg(5), gcol[:, perm], pad],
                        axis=1).astype(BF16)
    bg = jnp.zeros((1, LANES), F32).at[0, 0:N_GATES].set(b_gate[perm])
    kv_of = (0, 0, 1, 1)
    pe_rows, w_halves = [], []
    eye4 = jnp.eye(4, dtype=F32)
    for half in range(2):
        ls = slice(half * CMP_D, (half + 1) * CMP_D)
        pe_rows.append(jnp.stack([pe[kv_of[j], ls, :] for j in range(4)], axis=1).reshape(CHUNK_W))
        w1r = jnp.stack([w1[kv_of[j]].reshape(CMP_L, HEAD_DIM, HEAD_DIM)[ls] for j in range(4)], axis=1)
        wexp = (w1r[:, :, :, None, :] * eye4[None, :, None, :, None]).reshape(CHUNK_W, KV_W)
        w_halves.append(wexp.astype(BF16))
    w2bd = (eye4[:, None, :, None] * jnp.stack([w2[kv_of[j]] for j in range(4)])[:, :, None, :]).reshape(KV_W, KV_W)
    return w, bg, jnp.stack(pe_rows), w_halves[0], w_halves[1], w2bd.astype(BF16)


def kernel(x_prompt, x_sample, cache_fox_kv, cache_fox_logf, cache_mla, cache_nsa_cmp, cache_nsa_slc,
           state_nsa_win, page_table, norm_w, final_norm, ffn_w_gate, ffn_w_up, ffn_w_down,
           ev_w_in, ev_b_f, mla_q_norm, mla_w_q_up, mla_kv_norm, mla_w_uk, mla_w_uv, ev_w_out,
           od_w_in, od_b_gate, nsa_cmp_pe, nsa_cmp_w1, nsa_cmp_w2, od_w_out):
    b, t, _ = x_prompt.shape
    bd, td, _ = x_sample.shape
    depth = norm_w.shape[0]
    n_pages = page_table.shape[1]
    past_len = n_pages * PAGE
    n_p, n_s = b * t, bd * td
    n = n_p + n_s
    wbuf = state_nsa_win.shape[2]

    tm = math.gcd(512, n_s)
    tf = FF_DIM // 2
    tm_ffn = tm
    tq = min(256, t)
    tk = min(512, t)
    ns_pad = -(-(-(-(past_len + td) // SEL_L)) // LANES) * LANES

    rope_mla = _rope_tables(t, past_len, td, tm, MLA_THETA, MLA_ROPE, MLA_ROPE)
    rope_nsa = _rope_tables(t, past_len, td, tm, ROPE_THETA, ROPE_DIM, HEAD_DIM)
    mimp_p = _imp_matrix(t // SEL_L, t // CMP_D)
    mimp_s = _imp_matrix(ns_pad, past_len // CMP_D).T
    expand_p = _expand_matrix(t // SEL_L, t)
    expand_s = _expand_matrix(ns_pad, past_len)
    eye_q = jnp.eye(tq, dtype=BF16)

    wg16, wu16, wd16 = ffn_w_gate.astype(BF16), ffn_w_up.astype(BF16), ffn_w_down.astype(BF16)
    tok_minor = lambda c: jnp.transpose(c, (0, 1, 3, 4, 5, 2)).reshape(c.shape[0], c.shape[1], KV_W, c.shape[2])
    cache_kv2 = tok_minor(cache_fox_kv)
    cache_lft = jnp.swapaxes(cache_fox_logf, 2, 3)
    cache_lat2 = jnp.swapaxes(cache_mla, 2, 3)
    cache_cmp2 = tok_minor(cache_nsa_cmp)
    cache_slc2 = tok_minor(cache_nsa_slc)
    state2 = tok_minor(state_nsa_win)

    h = jnp.concatenate([x_prompt.reshape(n_p, D_MODEL), x_sample.reshape(n_s, D_MODEL)], axis=0)
    row = lambda v: v.reshape(1, -1)
    kv_shape = (2, FOX_KV_HEADS, HEAD_DIM)
    outs = {k: [] for k in ("fkv_p", "fkv_s", "flf_p", "flf_s", "mla_p", "mla_s",
                            "cmp_p", "cmp_s", "slc_p", "slc_s", "win_p", "win_s")}
    for li in range(depth):
        e = li // 2
        h = _ffn(h, row(norm_w[li, 0]), wg16[li, 0], wu16[li, 0], wd16[li, 0], tm_ffn, tf)
        if li % 2 == 0:
            w, bf, wq, wuk_bd, wuv = _prep_even(ev_w_in[e], ev_b_f[e], mla_w_q_up[e], mla_w_uk[e], mla_w_uv[e])
            proj = functools.partial(_even_proj, h, row(norm_w[li, 1]), w, bf, row(mla_q_norm[e]), wq, wuk_bd,
                                     row(mla_kv_norm[e]), rope_mla, tm, t)
            qf, kv32, kv16, logf, qm, lat32, lat16 = proj(0, n_p, False)
            qf_s, kv32_s, _, logf_s, qm_s, lat32_s, _ = proj(n_p // tm, n_s, True)
            csum, cst = _csum(logf, b, t, tk)
            o_p = _even_attn(qf, qm, csum, kv16, lat16, cst, wuv, b, t, tq, tk)
            o_s = _even_dec(page_table, qf_s, qm_s, kv32_s, lat32_s, logf_s, wuv, cache_kv2, cache_lft, cache_lat2,
                            e, 0, bd, td)
            h = _out_proj(h, o_p, o_s, ev_w_out[e].astype(BF16), tm)
            outs["fkv_p"].append(kv32.reshape((b, t) + kv_shape))
            outs["fkv_s"].append(kv32_s.reshape((bd, td) + kv_shape))
            outs["flf_p"].append(logf.reshape(b, t, FOX_HEADS))
            outs["flf_s"].append(logf_s.reshape(bd, td, FOX_HEADS))
            outs["mla_p"].append(lat32.reshape(b, t, MLA_LAT))
            outs["mla_s"].append(lat32_s.reshape(bd, td, MLA_LAT))
        else:
            w, bg, pe2, wlo, whi, w2bd = _prep_odd(od_w_in[e], od_b_gate[e], nsa_cmp_pe[e], nsa_cmp_w1[e],
                                                   nsa_cmp_w2[e])
            proj = functools.partial(_odd_proj, h, row(norm_w[li, 1]), w, bg, rope_nsa, tm, t)
            q, cmp32, slc32, wn32, slc16, wn16, gates = proj(0, n_p, False)
            q_s, cmp32_s, slc32_s, wn32_s, _, _, gates_s = proj(n_p // tm, n_s, True)
            kvc = _compress_prompt(cmp32.reshape(n_p // CMP_D, CHUNK_W), pe2, wlo, whi, w2bd, b, t // CMP_D)
            o_p = _nsa_prompt(q, gates, kvc, slc16, wn16, mimp_p, eye_q, expand_p, b, t, tq, tk)
            ocmp, mask = _nsa_dec1(page_table, q_s, pe2, wlo, whi, w2bd, mimp_s, expand_s, cache_cmp2,
                                   e, 0, bd, td, ns_pad)
            o_s, nst = _nsa_dec2(page_table, q_s, gates_s, ocmp, mask, slc32_s, wn32_s, state2, cache_slc2,
                                 e, 0, bd, td)
            h = _out_proj(h, o_p, o_s, od_w_out[e].astype(BF16), tm)
            wlen = min(WINDOW, t)
            outs["cmp_p"].append(cmp32.reshape((b, t) + kv_shape))
            outs["cmp_s"].append(cmp32_s.reshape((bd, td) + kv_shape))
            outs["slc_p"].append(slc32.reshape((b, t) + kv_shape))
            outs["slc_s"].append(slc32_s.reshape((bd, td) + kv_shape))
            outs["win_p"].append(wn32.reshape((b, t) + kv_shape)[:, t - wlen:])
            outs["win_s"].append(jnp.transpose(nst.reshape((bd,) + kv_shape + (wbuf,)), (0, 4, 1, 2, 3)))
        h = _ffn(h, row(norm_w[li, 2]), wg16[li, 1], wu16[li, 1], wd16[li, 1], tm_ffn, tf)
    y_p = _final_norm(h, row(final_norm), tm, 0, n_p)
    y_s = _final_norm(h, row(final_norm), tm, n_p // tm, n_s)
    st = lambda k: jnp.stack(outs[k])
    return (y_p.reshape(b, t, D_MODEL), y_s.reshape(bd, td, D_MODEL),
            st("fkv_p"), st("fkv_s"), st("flf_p"), st("flf_s"), st("mla_p"), st("mla_s"),
            st("cmp_p"), st("cmp_s"), st("slc_p"), st("slc_s"), st("win_p"), st("win_s"))
```

```python
import functools
import math

import numpy as np
import jax
import jax.numpy as jnp
from jax import lax
from jax.experimental import pallas as pl
from jax.experimental.pallas import tpu as pltpu

F32 = jnp.float32
BF16 = jnp.bfloat16

D_MODEL = 1024
HEAD_DIM = 64
FOX_HEADS = 8
FOX_KV_HEADS = 2
MLA_HEADS = 8
MLA_Q_LORA = 256
MLA_KV_LORA = 128
MLA_NOPE = 64
MLA_ROPE = 32
MLA_V = 64
MLA_THETA = 10000.0
NSA_HEADS = 16
NSA_KV_HEADS = 2
CMP_L = 32
CMP_D = 16
SEL_L = 64
TOP_N = 16
WINDOW = 512
ROPE_THETA = 500000.0
ROPE_DIM = HEAD_DIM // 4
FF_DIM = 2816
EPS = 1e-6
FORCE_SCORE = 1e9
NEG_INF = -1e30
PAGE = 128

LANES = 128
MLA_LAT = MLA_KV_LORA + MLA_ROPE
MLA_QK = MLA_NOPE + MLA_ROPE
LOG2E = math.log2(math.e)
FOX_SCALE = HEAD_DIM ** -0.5 * LOG2E
MLA_SCALE = MLA_QK ** -0.5 * LOG2E
KV_W = 2 * FOX_KV_HEADS * HEAD_DIM
CHUNK_W = CMP_D * KV_W
VMEM_LIMIT = 56 * 1024 * 1024


def _cp(sem):
    return pltpu.CompilerParams(dimension_semantics=sem, vmem_limit_bytes=VMEM_LIMIT)


def _nt(a, b, precision=None):
    return lax.dot_general(a, b, (((1,), (1,)), ((), ())), precision=precision,
                           preferred_element_type=F32)


def _mm(a, b, precision=None):
    return jnp.dot(a, b, precision=precision, preferred_element_type=F32)


def _rms_val(x, g):
    return (x * lax.rsqrt(jnp.mean(x * x, axis=-1, keepdims=True) + EPS)) * g


def _log_sigmoid(x):
    return -(jnp.maximum(-x, 0.0) + jnp.log1p(jnp.exp(-jnp.abs(x))))


def _gelu_tanh(x):
    return x * (0.5 * (1.0 + jnp.tanh(math.sqrt(2.0 / math.pi) * (x + 0.044715 * (x * x * x)))))


def _iota(shape, dim):
    return lax.broadcasted_iota(jnp.int32, shape, dim)


def _ffn_kernel(x_ref, g_ref, wg_ref, wu_ref, wd_ref, o_ref, xn_ref, acc_ref):
    j = pl.program_id(1)

    @pl.when(j == 0)
    def _():
        xn_ref[...] = _rms_val(x_ref[...], g_ref[...]).astype(BF16)
        acc_ref[...] = jnp.zeros_like(acc_ref)

    xn = xn_ref[...]
    g = _mm(xn, wg_ref[...])
    u = _mm(xn, wu_ref[...])
    a = (g * jax.nn.sigmoid(g)) * u
    acc_ref[...] += _mm(a.astype(BF16), wd_ref[...])

    @pl.when(j == pl.num_programs(1) - 1)
    def _():
        o_ref[...] = x_ref[...] + 0.5 * acc_ref[...]


def _ffn(h, g, wg, wu, wd, tm, tf):
    n = h.shape[0]
    return pl.pallas_call(
        _ffn_kernel,
        grid=(n // tm, FF_DIM // tf),
        in_specs=[
            pl.BlockSpec((tm, D_MODEL), lambda i, j: (i, 0)),
            pl.BlockSpec((1, D_MODEL), lambda i, j: (0, 0)),
            pl.BlockSpec((D_MODEL, tf), lambda i, j: (0, j)),
            pl.BlockSpec((D_MODEL, tf), lambda i, j: (0, j)),
            pl.BlockSpec((tf, D_MODEL), lambda i, j: (j, 0)),
        ],
        out_specs=pl.BlockSpec((tm, D_MODEL), lambda i, j: (i, 0)),
        out_shape=jax.ShapeDtypeStruct((n, D_MODEL), F32),
        scratch_shapes=[pltpu.VMEM((tm, D_MODEL), BF16), pltpu.VMEM((tm, D_MODEL), F32)],
        compiler_params=_cp(("parallel", "arbitrary")),
        name="ffn",
    )(h, g, wg, wu, wd)


def _out_proj_kernel(h_ref, op_ref, os_ref, w_ref, o_ref, *, npt):
    i = pl.program_id(0)

    @pl.when(i < npt)
    def _():
        o_ref[...] = h_ref[...] + _mm(op_ref[...], w_ref[...])

    @pl.when(i >= npt)
    def _():
        o_ref[...] = h_ref[...] + _mm(os_ref[...].astype(BF16), w_ref[...])


def _out_proj(h, o_p, o_s, w, tm):
    n = h.shape[0]
    npt = o_p.shape[0] // tm
    return pl.pallas_call(
        functools.partial(_out_proj_kernel, npt=npt),
        grid=(n // tm,),
        in_specs=[
            pl.BlockSpec((tm, D_MODEL), lambda i: (i, 0)),
            pl.BlockSpec((tm, D_MODEL), lambda i: (jnp.minimum(i, npt - 1), 0)),
            pl.BlockSpec((tm, D_MODEL), lambda i: (jnp.maximum(i - npt, 0), 0)),
            pl.BlockSpec((D_MODEL, D_MODEL), lambda i: (0, 0)),
        ],
        out_specs=pl.BlockSpec((tm, D_MODEL), lambda i: (i, 0)),
        out_shape=jax.ShapeDtypeStruct((n, D_MODEL), F32),
        compiler_params=_cp(("parallel",)),
        name="out_proj",
    )(h, o_p, o_s, w)


def _final_norm_kernel(x_ref, g_ref, o_ref):
    o_ref[...] = _rms_val(x_ref[...], g_ref[...])


def _final_norm(h, g, tm, tile0, n):
    return pl.pallas_call(
        _final_norm_kernel,
        grid=(n // tm,),
        in_specs=[pl.BlockSpec((tm, D_MODEL), lambda i: (tile0 + i, 0)),
                  pl.BlockSpec((1, D_MODEL), lambda i: (0, 0))],
        out_specs=pl.BlockSpec((tm, D_MODEL), lambda i: (i, 0)),
        out_shape=jax.ShapeDtypeStruct((n, D_MODEL), F32),
        compiler_params=_cp(("parallel",)),
        name="final_norm",
    )(h, g)


def _rope_tables(t_prompt, past_len, t_dec, tm, theta, rot_dim, period):
    half = rot_dim // 2
    pos = np.concatenate([np.arange(t_prompt), past_len + (np.arange(tm) % t_dec)]).astype(np.float64)
    inv = np.float64(theta) ** (-np.arange(half, dtype=np.float64) / half)
    ang = pos[:, None] * inv[None, :]
    cos, sin = np.cos(ang), np.sin(ang)
    lane = np.arange(LANES) % period
    c = np.ones((pos.shape[0], LANES))
    s1 = np.zeros((pos.shape[0], LANES))
    s2 = np.zeros((pos.shape[0], LANES))
    lo = lane < half
    hi = (lane >= half) & (lane < rot_dim)
    c[:, lo] = cos[:, lane[lo]]
    c[:, hi] = cos[:, lane[hi] - half]
    s1[:, hi] = sin[:, lane[hi] - half]
    s2[:, lo] = -sin[:, lane[lo]]
    return jnp.asarray(np.stack([c, s1, s2]), dtype=F32)


def _rope_chunk(x, rope_ref, half):
    return (x * rope_ref[0] + pltpu.roll(x, half, 1) * rope_ref[1]
            + pltpu.roll(x, LANES - half, 1) * rope_ref[2])


def _rope_spec(tm, t_prompt, sample):
    tiles = t_prompt // tm
    if sample:
        return pl.BlockSpec((3, tm, LANES), lambda i: (0, tiles, 0))
    return pl.BlockSpec((3, tm, LANES), lambda i: (0, i % tiles, 0))


EV_COLS = 1280
EV_FLG_LANE = 32


def _even_proj_kernel(x_ref, g_ref, win_ref, bf_ref, qn_ref, wqu_ref, wuk_ref, kvn_ref, rope_ref,
                      qf_ref, kv32_ref, kv16_ref, logf_ref, qm_ref, lat32_ref, lat16_ref):
    xn = _rms_val(x_ref[...], g_ref[...]).astype(BF16)
    z = _mm(xn, win_ref[...])
    for h in range(FOX_HEADS):
        qf_ref[h] = z[:, HEAD_DIM * h:HEAD_DIM * (h + 1)] * FOX_SCALE
    kv = z[:, 512:768]
    kv32_ref[...] = kv
    for j in range(4):
        kv16_ref[j] = kv[:, HEAD_DIM * j:HEAD_DIM * (j + 1)].astype(BF16)
    cq = _rms_val(z[:, 768:1024], qn_ref[...]).astype(BF16)
    qmm = _mm(cq, wqu_ref[...])
    qlat = _mm(qmm[:, :512].astype(BF16), wuk_ref[...]) * MLA_SCALE
    for h in range(MLA_HEADS):
        qm_ref[h, :, 0:MLA_KV_LORA] = qlat[:, MLA_KV_LORA * h:MLA_KV_LORA * (h + 1)]
    for c in range(2):
        pe = _rope_chunk(qmm[:, 512 + LANES * c:512 + LANES * (c + 1)], rope_ref, MLA_ROPE // 2) * MLA_SCALE
        for hh in range(4):
            qm_ref[4 * c + hh, :, MLA_KV_LORA:MLA_LAT] = pe[:, MLA_ROPE * hh:MLA_ROPE * (hh + 1)]
    ckv = _rms_val(z[:, 1024:1152], kvn_ref[...])
    last = z[:, 1152:1280]
    kpe = _rope_chunk(last, rope_ref, MLA_ROPE // 2)[:, 0:MLA_ROPE]
    lat32_ref[:, 0:MLA_KV_LORA] = ckv
    lat32_ref[:, MLA_KV_LORA:MLA_LAT] = kpe
    lat16_ref[:, 0:MLA_KV_LORA] = ckv.astype(BF16)
    lat16_ref[:, MLA_KV_LORA:MLA_LAT] = kpe.astype(BF16)
    logf_ref[...] = _log_sigmoid(last + bf_ref[...])[:, EV_FLG_LANE:EV_FLG_LANE + FOX_HEADS]


def _even_proj(h, g, w_in, b_f, q_norm, w_q_up, wuk_bd, kv_norm, rope, tm, t_prompt, tile0, n, sample):
    full = lambda shape: pl.BlockSpec(shape, lambda i: (0,) * len(shape))
    return pl.pallas_call(
        _even_proj_kernel,
        grid=(n // tm,),
        in_specs=[
            pl.BlockSpec((tm, D_MODEL), lambda i: (tile0 + i, 0)),
            full((1, D_MODEL)), full((D_MODEL, EV_COLS)), full((1, LANES)), full((1, MLA_Q_LORA)),
            full((MLA_Q_LORA, 768)), full((512, 1024)), full((1, MLA_KV_LORA)),
            _rope_spec(tm, t_prompt, sample),
        ],
        out_specs=[
            pl.BlockSpec((FOX_HEADS, tm, HEAD_DIM), lambda i: (0, i, 0)),
            pl.BlockSpec((tm, KV_W), lambda i: (i, 0)),
            pl.BlockSpec((4, tm, HEAD_DIM), lambda i: (0, i, 0)),
            pl.BlockSpec((tm, FOX_HEADS), lambda i: (i, 0)),
            pl.BlockSpec((MLA_HEADS, tm, MLA_LAT), lambda i: (0, i, 0)),
            pl.BlockSpec((tm, MLA_LAT), lambda i: (i, 0)),
            pl.BlockSpec((tm, MLA_LAT), lambda i: (i, 0)),
        ],
        out_shape=[
            jax.ShapeDtypeStruct((FOX_HEADS, n, HEAD_DIM), F32),
            jax.ShapeDtypeStruct((n, KV_W), F32),
            jax.ShapeDtypeStruct((4, n, HEAD_DIM), BF16),
            jax.ShapeDtypeStruct((n, FOX_HEADS), F32),
            jax.ShapeDtypeStruct((MLA_HEADS, n, MLA_LAT), F32),
            jax.ShapeDtypeStruct((n, MLA_LAT), F32),
            jax.ShapeDtypeStruct((n, MLA_LAT), BF16),
        ],
        compiler_params=_cp(("parallel",)),
        name="even_proj",
    )(h, g, w_in, b_f, q_norm, w_q_up, wuk_bd, kv_norm, rope)


CS_CHUNK = 128


def _csum_kernel(lf_ref, cs_ref, cst_ref, *, t, tk):
    r = _iota((CS_CHUNK, CS_CHUNK), 0)
    c = _iota((CS_CHUNK, CS_CHUNK), 1)
    tri = (c <= r).astype(F32)
    eye = (_iota((FOX_HEADS, FOX_HEADS), 0) == _iota((FOX_HEADS, FOX_HEADS), 1)).astype(F32)
    carry = jnp.zeros((1, FOX_HEADS), F32)
    per = tk // CS_CHUNK
    for k in range(t // CS_CHUNK):
        lf = lf_ref[k * CS_CHUNK:(k + 1) * CS_CHUNK, :]
        cs = _mm(tri, lf, lax.Precision.HIGHEST) + carry
        carry = cs[CS_CHUNK - 1:CS_CHUNK, :]
        cs_ref[k * CS_CHUNK:(k + 1) * CS_CHUNK, :] = cs
        cst_ref[k // per, :, (k % per) * CS_CHUNK:(k % per + 1) * CS_CHUNK] = _nt(eye, cs, lax.Precision.HIGHEST)


def _csum(logf, b, t, tk):
    return pl.pallas_call(
        functools.partial(_csum_kernel, t=t, tk=tk),
        grid=(b,),
        in_specs=[pl.BlockSpec((t, FOX_HEADS), lambda i: (i, 0))],
        out_specs=[pl.BlockSpec((t, FOX_HEADS), lambda i: (i, 0)),
                   pl.BlockSpec((None, t // tk, FOX_HEADS, tk), lambda i: (i, 0, 0, 0))],
        out_shape=[jax.ShapeDtypeStruct((b * t, FOX_HEADS), F32),
                   jax.ShapeDtypeStruct((b, t // tk, FOX_HEADS, tk), F32)],
        compiler_params=_cp(("parallel",)),
        name="fox_csum",
    )(logf)


def _lane_chunks(s):
    return [s[:, LANES * c:LANES * (c + 1)] for c in range(s.shape[1] // LANES)]


def _online_update(chunks, m_ref, l_ref, acc_ref, v, idx):
    m_prev = m_ref[idx]
    mx = chunks[0]
    for c in chunks[1:]:
        mx = jnp.maximum(mx, c)
    m_new = jnp.maximum(m_prev, jnp.broadcast_to(jnp.max(mx, axis=-1, keepdims=True), m_prev.shape))
    alpha = jnp.exp2(m_prev - m_new)
    ps = [jnp.exp2(c - m_new) for c in chunks]
    lsum = ps[0]
    for p in ps[1:]:
        lsum = lsum + p
    l_ref[idx] = alpha * l_ref[idx] + lsum
    dv = acc_ref.shape[-1]
    acc_ref[idx] = alpha[:, :dv] * acc_ref[idx] + _mm(jnp.concatenate(ps, axis=1).astype(BF16), v)
    m_ref[idx] = m_new


def _online_finish(l_ref, acc_ref, idx):
    return acc_ref[idx] / jnp.sum(l_ref[idx], axis=-1, keepdims=True)


def _even_attn_kernel(qf_ref, qm_ref, cs_ref, kv_ref, lat_ref, cst_ref, wuv_ref, o_ref,
                      mf_ref, lf_ref, af_ref, mm_ref, lm_ref, am_ref, *, tq, tk):
    qi = pl.program_id(1)
    qs = qi * tq
    rq = FOX_HEADS // FOX_KV_HEADS
    mf_ref[...] = jnp.full_like(mf_ref, NEG_INF)
    lf_ref[...] = jnp.zeros_like(lf_ref)
    af_ref[...] = jnp.zeros_like(af_ref)
    mm_ref[...] = jnp.full_like(mm_ref, NEG_INF)
    lm_ref[...] = jnp.zeros_like(lm_ref)
    am_ref[...] = jnp.zeros_like(am_ref)
    cs = cs_ref[...] * LOG2E
    cqb = [jnp.broadcast_to(cs[:, h:h + 1], (tq, LANES)) for h in range(FOX_HEADS)]
    qg = [qf_ref[rq * g:rq * (g + 1)].reshape(rq * tq, HEAD_DIM).astype(BF16) for g in range(FOX_KV_HEADS)]
    qm = qm_ref[...].reshape(MLA_HEADS * tq, MLA_LAT).astype(BF16)
    qpos = qs + _iota((tq, LANES), 0)

    def step(ks, width, masked):
        ks = pl.multiple_of(ks, tq)
        nch = width // LANES
        if masked:
            keep = [jnp.where((ks + LANES * c + _iota((tq, LANES), 1)) <= qpos, 0.0, NEG_INF) for c in range(nch)]
        ck = jnp.concatenate([cst_ref[ks // tq + i] for i in range(width // tq)], axis=1) * LOG2E
        for g in range(FOX_KV_HEADS):
            k = kv_ref[g, pl.ds(ks, width), :]
            v = kv_ref[FOX_KV_HEADS + g, pl.ds(ks, width), :]
            s = _nt(qg[g], k)
            chunks = []
            for c in range(nch):
                parts = []
                for r in range(rq):
                    h = rq * g + r
                    bias = cqb[h] - ck[h:h + 1, LANES * c:LANES * (c + 1)]
                    if masked:
                        bias = bias + keep[c]
                    parts.append(s[r * tq:(r + 1) * tq, LANES * c:LANES * (c + 1)] + bias)
                chunks.append(jnp.concatenate(parts, axis=0))
            _online_update(chunks, mf_ref, lf_ref, af_ref, v, g)
        lat = lat_ref[pl.ds(ks, width), :]
        chunks = _lane_chunks(_nt(qm, lat))
        if masked:
            chunks = [(chunks[c].reshape(MLA_HEADS, tq, LANES) + keep[c][None]).reshape(MLA_HEADS * tq, LANES)
                      for c in range(nch)]
        _online_update(chunks, mm_ref, lm_ref, am_ref, lat[:, :MLA_KV_LORA], 0)

    n_full = qs // tk

    def full_body(j, c):
        step(j * tk, tk, False)
        return c

    def part_body(i, c):
        step(n_full * tk + i * tq, tq, False)
        return c

    lax.fori_loop(0, n_full, full_body, 0)
    lax.fori_loop(0, (qs - n_full * tk) // tq, part_body, 0)
    step(qs, tq, True)

    for g in range(FOX_KV_HEADS):
        o = _online_finish(lf_ref, af_ref, g)
        for r in range(rq):
            h = rq * g + r
            o_ref[:, HEAD_DIM * h:HEAD_DIM * (h + 1)] = o[r * tq:(r + 1) * tq].astype(BF16)
    ol = _online_finish(lm_ref, am_ref, 0).astype(BF16)
    base = FOX_HEADS * HEAD_DIM
    for h in range(MLA_HEADS):
        om = _mm(ol[h * tq:(h + 1) * tq], wuv_ref[h])
        o_ref[:, base + MLA_V * h:base + MLA_V * (h + 1)] = om.astype(BF16)


def _even_attn(qf, qm, csum, kv16, lat16, cst, wuv, b, t, tq, tk):
    nq = t // tq
    rq = FOX_HEADS // FOX_KV_HEADS
    return pl.pallas_call(
        functools.partial(_even_attn_kernel, tq=tq, tk=tk),
        grid=(b, nq),
        in_specs=[
            pl.BlockSpec((FOX_HEADS, tq, HEAD_DIM), lambda i, j: (0, i * nq + j, 0)),
            pl.BlockSpec((MLA_HEADS, tq, MLA_LAT), lambda i, j: (0, i * nq + j, 0)),
            pl.BlockSpec((tq, FOX_HEADS), lambda i, j: (i * nq + j, 0)),
            pl.BlockSpec((4, t, HEAD_DIM), lambda i, j: (0, i, 0)),
            pl.BlockSpec((t, MLA_LAT), lambda i, j: (i, 0)),
            pl.BlockSpec((None, t // tq, FOX_HEADS, tq), lambda i, j: (i, 0, 0, 0)),
            pl.BlockSpec((MLA_HEADS, MLA_KV_LORA, MLA_V), lambda i, j: (0, 0, 0)),
        ],
        out_specs=pl.BlockSpec((tq, D_MODEL), lambda i, j: (i * nq + j, 0)),
        out_shape=jax.ShapeDtypeStruct((b * t, D_MODEL), BF16),
        scratch_shapes=[
            pltpu.VMEM((FOX_KV_HEADS, rq * tq, LANES), F32), pltpu.VMEM((FOX_KV_HEADS, rq * tq, LANES), F32),
            pltpu.VMEM((FOX_KV_HEADS, rq * tq, HEAD_DIM), F32),
            pltpu.VMEM((1, MLA_HEADS * tq, LANES), F32), pltpu.VMEM((1, MLA_HEADS * tq, LANES), F32),
            pltpu.VMEM((1, MLA_HEADS * tq, MLA_KV_LORA), F32),
        ],
        compiler_params=_cp(("parallel", "arbitrary")),
        name="even_attn_prompt",
    )(qf, qm, csum, kv16, lat16, cst, wuv)


def _split3(x):
    hi = x.astype(BF16)
    r1 = x - hi.astype(F32)
    mid = r1.astype(BF16)
    return hi, mid, (r1 - mid.astype(F32)).astype(BF16)


def _mm_exact_l(x, w):
    hi, mid, lo = _split3(x)
    return _mm(hi, w) + _mm(mid, w) + _mm(lo, w)


def _mm_exact_r(w, x):
    hi, mid, lo = _split3(x)
    return _mm(w, hi) + _mm(w, mid) + _mm(w, lo)


def _joint_softmax(s_past, s_new, pv_past, pv_new):
    mx = jnp.maximum(jnp.max(s_past, axis=-1, keepdims=True), jnp.max(s_new, axis=-1, keepdims=True))
    p1 = jnp.exp2(s_past - mx)
    p2 = jnp.exp2(s_new - mx)
    den = jnp.sum(p1, axis=-1, keepdims=True) + jnp.sum(p2, axis=-1, keepdims=True)
    return (pv_past(p1.astype(BF16)) + pv_new(p2.astype(BF16))) / den


def _even_dec_kernel(pt_ref, qf_ref, qm_ref, kvn_ref, latn_ref, lfn_ref, wuv_ref, tri_ref, ones_ref, pre_ref,
                     *rest, n_pages, td):
    kv_refs = rest[:n_pages]
    lf_refs = rest[n_pages:2 * n_pages]
    lat_refs = rest[2 * n_pages:3 * n_pages]
    o_ref = rest[3 * n_pages]
    rq = FOX_HEADS // FOX_KV_HEADS
    rows = rq * td

    lf = jnp.concatenate([lf_refs[i][...] for i in range(n_pages)], axis=0)
    cum = _mm_exact_l(lf, tri_ref[...])
    tot = _mm_exact_l(lf, ones_ref[...])
    pre = _mm_exact_r(pre_ref[...], tot)
    ck = (cum + pre) * LOG2E
    last = FOX_HEADS * (n_pages - 1)
    past_total = (pre + tot)[last:last + FOX_HEADS] * LOG2E

    tri_n = (_iota((td, td), 1) <= _iota((td, td), 0)).astype(F32)
    eye = (_iota((FOX_HEADS, FOX_HEADS), 0) == _iota((FOX_HEADS, FOX_HEADS), 1)).astype(F32)
    csn = _mm(tri_n, lfn_ref[...], lax.Precision.HIGHEST) * LOG2E
    ckn = _nt(eye, csn, lax.Precision.HIGHEST) + past_total[:, 0:td]
    cq = [csn[:, h:h + 1] + past_total[h:h + 1, :] for h in range(FOX_HEADS)]
    kvn = kvn_ref[...].astype(BF16)
    keep = _iota((rows, td), 1) <= _iota((rows, td), 0) % td

    def rows_of(refs, lo, hi):
        return jnp.concatenate([refs[i][lo:hi, :].astype(BF16) for i in range(n_pages)], axis=1)

    for g in range(FOX_KV_HEADS):
        q = qf_ref[rq * g:rq * (g + 1)].reshape(rows, HEAD_DIM).astype(BF16)
        kt = rows_of(kv_refs, HEAD_DIM * g, HEAD_DIM * (g + 1))
        vo = (FOX_KV_HEADS + g) * HEAD_DIM
        vt = rows_of(kv_refs, vo, vo + HEAD_DIM)
        s = _mm(q, kt)
        parts = []
        for r in range(rq):
            h = rq * g + r
            bias = jnp.concatenate([cq[h] - ck[FOX_HEADS * i + h:FOX_HEADS * i + h + 1, :]
                                    for i in range(n_pages)], axis=1)
            parts.append(s[r * td:(r + 1) * td] + bias)
        s = jnp.concatenate(parts, axis=0)
        sn = _nt(q, kvn[:, HEAD_DIM * g:HEAD_DIM * (g + 1)])
        sn = jnp.concatenate([sn[r * td:(r + 1) * td] + (cq[rq * g + r][:, 0:td] - ckn[rq * g + r:rq * g + r + 1, :])
                              for r in range(rq)], axis=0)
        sn = jnp.where(keep, sn, NEG_INF)
        o = _joint_softmax(s, sn, lambda p: _nt(p, vt), lambda p: _mm(p, kvn[:, vo:vo + HEAD_DIM]))
        for r in range(rq):
            h = rq * g + r
            o_ref[:, HEAD_DIM * h:HEAD_DIM * (h + 1)] = o[r * td:(r + 1) * td]

    latt = rows_of(lat_refs, 0, MLA_LAT)
    latn = latn_ref[...].astype(BF16)
    qm = qm_ref[...].reshape(MLA_HEADS * td, MLA_LAT).astype(BF16)
    keep_m = _iota((MLA_HEADS * td, td), 1) <= _iota((MLA_HEADS * td, td), 0) % td
    sn = jnp.where(keep_m, _nt(qm, latn), NEG_INF)
    ol = _joint_softmax(_mm(qm, latt), sn, lambda p: _nt(p, latt[:MLA_KV_LORA]),
                        lambda p: _mm(p, latn[:, :MLA_KV_LORA])).astype(BF16)
    base = FOX_HEADS * HEAD_DIM
    for h in range(MLA_HEADS):
        o_ref[:, base + MLA_V * h:base + MLA_V * (h + 1)] = _mm(ol[h * td:(h + 1) * td], wuv_ref[h])


def _even_dec(page_table, qf, qm, kv32, lat32, logf, wuv, cache_kv, cache_lft, cache_lat, e, n_p, bd, td):
    n_pages = page_table.shape[1]
    off = n_p // td
    nr = n_pages * FOX_HEADS
    tri = jnp.asarray(np.triu(np.ones((PAGE, PAGE), np.float32)), dtype=BF16)
    ones = jnp.ones((PAGE, PAGE), BF16)
    r = np.arange(nr)
    pre = ((r[None, :] // FOX_HEADS < r[:, None] // FOX_HEADS) & (r[None, :] % FOX_HEADS == r[:, None] % FOX_HEADS))
    pre = jnp.asarray(pre.astype(np.float32), dtype=BF16)

    def page_spec(shape, i):
        return pl.BlockSpec((None, None) + shape, lambda b, pt: (e, pt[b, i], 0, 0))

    full = lambda shape: pl.BlockSpec(shape, lambda b, pt: (0,) * len(shape))
    in_specs = [
        pl.BlockSpec((FOX_HEADS, td, HEAD_DIM), lambda b, pt: (0, off + b, 0)),
        pl.BlockSpec((MLA_HEADS, td, MLA_LAT), lambda b, pt: (0, off + b, 0)),
        pl.BlockSpec((td, KV_W), lambda b, pt: (off + b, 0)),
        pl.BlockSpec((td, MLA_LAT), lambda b, pt: (off + b, 0)),
        pl.BlockSpec((td, FOX_HEADS), lambda b, pt: (off + b, 0)),
        full((MLA_HEADS, MLA_KV_LORA, MLA_V)), full((PAGE, PAGE)), full((PAGE, PAGE)), full((nr, nr)),
    ]
    in_specs += [page_spec((KV_W, PAGE), i) for i in range(n_pages)]
    in_specs += [page_spec((FOX_HEADS, PAGE), i) for i in range(n_pages)]
    in_specs += [page_spec((MLA_LAT, PAGE), i) for i in range(n_pages)]
    grid_spec = pltpu.PrefetchScalarGridSpec(
        num_scalar_prefetch=1,
        grid=(bd,),
        in_specs=in_specs,
        out_specs=pl.BlockSpec((td, D_MODEL), lambda b, pt: (b, 0)),
    )
    return pl.pallas_call(
        functools.partial(_even_dec_kernel, n_pages=n_pages, td=td),
        grid_spec=grid_spec,
        out_shape=jax.ShapeDtypeStruct((bd * td, D_MODEL), F32),
        compiler_params=_cp(("parallel",)),
        name="even_attn_decode",
    )(page_table, qf, qm, kv32, lat32, logf, wuv, tri, ones, pre,
      *([cache_kv] * n_pages), *([cache_lft] * n_pages), *([cache_lat] * n_pages))


OD_COLS = 1920
N_GATES = 3 * NSA_HEADS


def _odd_proj_kernel(x_ref, g_ref, win_ref, bg_ref, rope_ref,
                     q_ref, cmp_ref, slc_ref, wn_ref, slc16_ref, wn16_ref, gate_ref):
    xn = _rms_val(x_ref[...], g_ref[...]).astype(BF16)
    z = _mm(xn, win_ref[...])
    half = ROPE_DIM // 2
    for c in range(NSA_HEADS // 2):
        qc = _rope_chunk(z[:, LANES * c:LANES * (c + 1)], rope_ref, half) * FOX_SCALE
        q_ref[2 * c] = qc[:, :HEAD_DIM]
        q_ref[2 * c + 1] = qc[:, HEAD_DIM:]
    ko, vo = NSA_HEADS * HEAD_DIM, NSA_HEADS * HEAD_DIM + 3 * LANES
    for n, (r32, r16) in enumerate(((cmp_ref, None), (slc_ref, slc16_ref), (wn_ref, wn16_ref))):
        k = _rope_chunk(z[:, ko + LANES * n:ko + LANES * (n + 1)], rope_ref, half)
        v = z[:, vo + LANES * n:vo + LANES * (n + 1)]
        r32[:, 0:LANES] = k
        r32[:, LANES:2 * LANES] = v
        if r16 is not None:
            for g in range(NSA_KV_HEADS):
                r16[g] = k[:, HEAD_DIM * g:HEAD_DIM * (g + 1)].astype(BF16)
                r16[NSA_KV_HEADS + g] = v[:, HEAD_DIM * g:HEAD_DIM * (g + 1)].astype(BF16)
    gate_ref[...] = jax.nn.sigmoid(z[:, vo + 3 * LANES:vo + 4 * LANES] + bg_ref[...])[:, 0:N_GATES]


def _odd_proj(h, g, w_in, b_gate, rope, tm, t_prompt, tile0, n, sample):
    full = lambda shape: pl.BlockSpec(shape, lambda i: (0,) * len(shape))
    row = lambda w: pl.BlockSpec((tm, w), lambda i: (i, 0))
    hm = lambda k: pl.BlockSpec((k, tm, HEAD_DIM), lambda i: (0, i, 0))
    return pl.pallas_call(
        _odd_proj_kernel,
        grid=(n // tm,),
        in_specs=[pl.BlockSpec((tm, D_MODEL), lambda i: (tile0 + i, 0)), full((1, D_MODEL)),
                  full((D_MODEL, OD_COLS)), full((1, LANES)), _rope_spec(tm, t_prompt, sample)],
        out_specs=[hm(NSA_HEADS), row(KV_W), row(KV_W), row(KV_W), hm(4), hm(4), row(N_GATES)],
        out_shape=[
            jax.ShapeDtypeStruct((NSA_HEADS, n, HEAD_DIM), F32),
            jax.ShapeDtypeStruct((n, KV_W), F32), jax.ShapeDtypeStruct((n, KV_W), F32),
            jax.ShapeDtypeStruct((n, KV_W), F32),
            jax.ShapeDtypeStruct((4, n, HEAD_DIM), BF16), jax.ShapeDtypeStruct((4, n, HEAD_DIM), BF16),
            jax.ShapeDtypeStruct((n, N_GATES), F32),
        ],
        compiler_params=_cp(("parallel",)),
        name="odd_proj",
    )(h, g, w_in, b_gate, rope)


def _compress_rows(ch, pe_ref, wlo_ref, whi_ref, w2_ref):
    a = _mm((ch + pe_ref[0:1, :]).astype(BF16), wlo_ref[...])
    b = _mm((ch + pe_ref[1:2, :]).astype(BF16), whi_ref[...])
    hid = a + pltpu.roll(b, ch.shape[0] - 1, 0)
    return _mm(_gelu_tanh(hid).astype(BF16), w2_ref[...])


def _compress_prompt_kernel(ch_ref, pe_ref, wlo_ref, whi_ref, w2_ref, o_ref):
    o_ref[...] = _compress_rows(ch_ref[...], pe_ref, wlo_ref, whi_ref, w2_ref).astype(BF16)


def _compress_prompt(chunks, pe, wlo, whi, w2, b, rows):
    full = lambda shape: pl.BlockSpec(shape, lambda i: (0,) * len(shape))
    return pl.pallas_call(
        _compress_prompt_kernel,
        grid=(b,),
        in_specs=[pl.BlockSpec((rows, CHUNK_W), lambda i: (i, 0)),
                  full((2, CHUNK_W)), full((CHUNK_W, KV_W)), full((CHUNK_W, KV_W)), full((KV_W, KV_W))],
        out_specs=pl.BlockSpec((rows, KV_W), lambda i: (i, 0)),
        out_shape=jax.ShapeDtypeStruct((b * rows, KV_W), BF16),
        compiler_params=_cp(("parallel",)),
        name="nsa_compress_prompt",
    )(chunks, pe, wlo, whi, w2)


def _cmp_branch(q, kc, vc, qpos_col, heads, t):
    ncp = kc.shape[0]
    s = _nt(q, kc).reshape(heads, t, ncp)
    mc = (_iota((t, ncp), 1) * CMP_D + (CMP_L - 1)) <= qpos_col
    s = jnp.where(mc[None], s, NEG_INF)
    e = jnp.exp2(s - jnp.max(s, axis=-1, keepdims=True))
    p = jnp.where(mc[None], e / jnp.sum(e, axis=-1, keepdims=True), 0.0)
    o = _mm(p.reshape(heads * t, ncp).astype(BF16), vc)
    return o, jnp.sum(p, axis=0)


def _select_blocks(imp_t, qpos_row, sc_ref, n_real):
    ns, c = imp_t.shape
    blk = _iota((ns, c), 0)
    cur = qpos_row // SEL_L
    forced = (blk == 0) | (blk == cur) | (blk == cur - 1)
    valid = (blk * SEL_L <= qpos_row) & (blk < n_real)
    score = jnp.where(valid, jnp.where(forced, FORCE_SCORE, imp_t), -1.0)
    score = jnp.where(blk < n_real, score, -2.0)
    sc_ref[...] = score

    def body(i, rank):
        row = sc_ref[pl.ds(i, 1), :]
        ahead = (row > score) | ((row == score) & (i < blk))
        return rank + ahead.astype(F32)

    rank = lax.fori_loop(0, n_real, body, jnp.zeros((ns, c), F32))
    return (rank < float(min(TOP_N, n_real))).astype(F32)


def _select_blocks_rows(imp, qpos_col, n_real):
    c, ns = imp.shape
    blk = _iota((c, ns), 1)
    cur = qpos_col // SEL_L
    forced = (blk == 0) | (blk == cur) | (blk == cur - 1)
    valid = (blk * SEL_L <= qpos_col) & (blk < n_real)
    score = jnp.where(valid, jnp.where(forced, FORCE_SCORE, imp), -1.0)
    score = jnp.where(blk < n_real, score, -2.0)
    rank = jnp.zeros((c, ns), F32)
    for i in range(n_real):
        col = score[:, i:i + 1]
        ahead = (col > score) | ((col == score) & (blk > i))
        rank = rank + ahead.astype(F32)
    return (rank < float(min(TOP_N, n_real))).astype(F32)


def _imp_matrix(ns_pad, ncp):
    ratio, span = SEL_L // CMP_D, CMP_L // CMP_D
    m = np.zeros((ns_pad, ncp), np.float32)
    for j in range(ns_pad):
        for a in range(ratio):
            for b in range(span):
                cc = j * ratio + a + b
                if cc < ncp:
                    m[j, cc] += 1.0
    return jnp.asarray(m)


def _expand_matrix(ns_pad, n_keys):
    m = (np.arange(n_keys)[None, :] // SEL_L == np.arange(ns_pad)[:, None]).astype(np.float32)
    return jnp.asarray(m, dtype=BF16)


def _nsa_prompt_kernel(q_ref, gate_ref, kvc_ref, slc_ref, wn_ref, mimp_ref, eye_ref, exp_ref, o_ref,
                       m_ref, l_ref, acc_ref, mask_ref, sc_ref, *, tq, tk, t):
    qi = pl.program_id(1)
    qs = qi * tq
    rq = NSA_HEADS // NSA_KV_HEADS
    ns = t // SEL_L
    qpos_col = qs + _iota((tq, 1), 0)
    qpos_row = qs + _iota((1, tq), 1)
    kvc = kvc_ref[...]
    gates = gate_ref[...]
    cdiag = jnp.where(_iota((tq, tq), 1) <= _iota((tq, tq), 0), 0.0, NEG_INF)
    wspan = min(WINDOW + tq, t)
    w0 = pl.multiple_of(jnp.minimum(jnp.maximum(qs - WINDOW, 0), t - wspan), tq)
    dwin = (qs + _iota((tq, wspan), 0)) - (w0 + _iota((tq, wspan), 1))
    bias_win = jnp.where((dwin >= 0) & (dwin < WINDOW), 0.0, NEG_INF)

    def masked_chunks(s, bias):
        out = []
        for c in range(s.shape[1] // LANES):
            sl = slice(LANES * c, LANES * (c + 1))
            out.append((s[:, sl].reshape(rq, tq, LANES) + bias[:, sl][None]).reshape(rq * tq, LANES))
        return out

    for g in range(NSA_KV_HEADS):
        q = q_ref[rq * g:rq * (g + 1)].reshape(rq * tq, HEAD_DIM).astype(BF16)
        kc = kvc[:, HEAD_DIM * g:HEAD_DIM * (g + 1)]
        vc = kvc[:, HEAD_DIM * (NSA_KV_HEADS + g):HEAD_DIM * (NSA_KV_HEADS + g + 1)]
        o_cmp, pgrp = _cmp_branch(q, kc, vc, qpos_col, rq, tq)
        imp_t = _nt(mimp_ref[...], pgrp, lax.Precision.HIGHEST)
        sel_t = _select_blocks(imp_t, qpos_row, sc_ref, ns)
        sel = _nt(eye_ref[...], sel_t.astype(BF16))
        selb = sel.astype(BF16)
        for jj in range(t // tq):
            hit = _mm(selb, exp_ref[:, jj * tq:(jj + 1) * tq])
            mask_ref[jj] = (1.0 - hit) * NEG_INF
        mask_ref[qi] = mask_ref[qi] + cdiag

        m_ref[...] = jnp.full_like(m_ref, NEG_INF)
        l_ref[...] = jnp.zeros_like(l_ref)
        acc_ref[...] = jnp.zeros_like(acc_ref)

        def slc_step(ks, width):
            ks = pl.multiple_of(ks, tq)
            k = slc_ref[g, pl.ds(ks, width), :]
            v = slc_ref[NSA_KV_HEADS + g, pl.ds(ks, width), :]
            bias = jnp.concatenate([mask_ref[ks // tq + i] for i in range(width // tq)], axis=1)
            _online_update(masked_chunks(_nt(q, k), bias), m_ref, l_ref, acc_ref, v, 0)

        n_full = qs // tk

        def full_body(j, c):
            slc_step(j * tk, tk)
            return c

        def part_body(i, c):
            slc_step(n_full * tk + i * tq, tq)
            return c

        lax.fori_loop(0, n_full, full_body, 0)
        lax.fori_loop(0, (qs - n_full * tk) // tq + 1, part_body, 0)
        o_slc = _online_finish(l_ref, acc_ref, 0)

        kw = wn_ref[g, pl.ds(w0, wspan), :]
        vw = wn_ref[NSA_KV_HEADS + g, pl.ds(w0, wspan), :]
        chunks = masked_chunks(_nt(q, kw), bias_win)
        mx = chunks[0]
        for c in chunks[1:]:
            mx = jnp.maximum(mx, c)
        mx = jnp.broadcast_to(jnp.max(mx, axis=-1, keepdims=True), mx.shape)
        ps = [jnp.exp2(c - mx) for c in chunks]
        den = ps[0]
        for p in ps[1:]:
            den = den + p
        o_win = _mm(jnp.concatenate(ps, axis=1).astype(BF16), vw) / jnp.sum(den, axis=-1, keepdims=True)

        for r in range(rq):
            h = rq * g + r
            rows = slice(r * tq, (r + 1) * tq)
            o = (gates[:, h:h + 1] * o_cmp[rows] + gates[:, NSA_HEADS + h:NSA_HEADS + h + 1] * o_slc[rows]
                 + gates[:, 2 * NSA_HEADS + h:2 * NSA_HEADS + h + 1] * o_win[rows])
            o_ref[:, HEAD_DIM * h:HEAD_DIM * (h + 1)] = o.astype(BF16)


def _nsa_prompt(q, gates, kvc, slc16, wn16, mimp, eye, expand, b, t, tq, tk):
    nq = t // tq
    rq = NSA_HEADS // NSA_KV_HEADS
    ncp = t // CMP_D
    ns = t // SEL_L
    return pl.pallas_call(
        functools.partial(_nsa_prompt_kernel, tq=tq, tk=tk, t=t),
        grid=(b, nq),
        in_specs=[
            pl.BlockSpec((NSA_HEADS, tq, HEAD_DIM), lambda i, j: (0, i * nq + j, 0)),
            pl.BlockSpec((tq, N_GATES), lambda i, j: (i * nq + j, 0)),
            pl.BlockSpec((ncp, KV_W), lambda i, j: (i, 0)),
            pl.BlockSpec((4, t, HEAD_DIM), lambda i, j: (0, i, 0)),
            pl.BlockSpec((4, t, HEAD_DIM), lambda i, j: (0, i, 0)),
            pl.BlockSpec((ns, ncp), lambda i, j: (0, 0)),
            pl.BlockSpec((tq, tq), lambda i, j: (0, 0)),
            pl.BlockSpec((ns, t), lambda i, j: (0, 0)),
        ],
        out_specs=pl.BlockSpec((tq, D_MODEL), lambda i, j: (i * nq + j, 0)),
        out_shape=jax.ShapeDtypeStruct((b * t, D_MODEL), BF16),
        scratch_shapes=[
            pltpu.VMEM((1, rq * tq, LANES), F32), pltpu.VMEM((1, rq * tq, LANES), F32),
            pltpu.VMEM((1, rq * tq, HEAD_DIM), F32),
            pltpu.VMEM((t // tq, tq, tq), F32),
            pltpu.VMEM((ns, tq), F32),
        ],
        compiler_params=_cp(("parallel", "arbitrary")),
        name="nsa_attn_prompt",
    )(q, gates, kvc, slc16, wn16, mimp, eye, expand)


def _nsa_dec1_kernel(pt_ref, q_ref, pe_ref, wlo_ref, whi_ref, w2_ref, mimp_ref, exp_ref, *rest,
                     n_pages, td, past_len, ns_pad):
    pages = rest[:n_pages]
    ocmp_ref, mask_ref, xs_ref = rest[n_pages:]
    rq = NSA_HEADS // NSA_KV_HEADS
    n_real = -(-(past_len + td) // SEL_L)
    nchunk = past_len // CMP_D
    for i in range(n_pages):
        x = pages[i][...].T
        xs_ref[0, PAGE * i:PAGE * (i + 1), :] = x[:, :LANES]
        xs_ref[1, PAGE * i:PAGE * (i + 1), :] = x[:, LANES:]
    a = jnp.zeros((nchunk, KV_W), F32)
    b = jnp.zeros((nchunk, KV_W), F32)
    for l in range(CMP_D):
        r = jnp.concatenate([xs_ref[0, pl.ds(l, nchunk, stride=CMP_D), :],
                             xs_ref[1, pl.ds(l, nchunk, stride=CMP_D), :]], axis=1)
        ws = slice(KV_W * l, KV_W * (l + 1))
        a = a + _mm((r + pe_ref[0:1, ws]).astype(BF16), wlo_ref[ws, :])
        b = b + _mm((r + pe_ref[1:2, ws]).astype(BF16), whi_ref[ws, :])
    hid = a + pltpu.roll(b, nchunk - 1, 0)
    kvc = _mm(_gelu_tanh(hid).astype(BF16), w2_ref[...]).astype(BF16)
    qpos_col = past_len + _iota((td, 1), 0)
    pgs = []
    for g in range(NSA_KV_HEADS):
        q = q_ref[rq * g:rq * (g + 1)].reshape(rq * td, HEAD_DIM).astype(BF16)
        kc = kvc[:, HEAD_DIM * g:HEAD_DIM * (g + 1)]
        vc = kvc[:, HEAD_DIM * (NSA_KV_HEADS + g):HEAD_DIM * (NSA_KV_HEADS + g + 1)]
        o_cmp, pgrp = _cmp_branch(q, kc, vc, qpos_col, rq, td)
        for r in range(rq):
            h = rq * g + r
            ocmp_ref[:, HEAD_DIM * h:HEAD_DIM * (h + 1)] = o_cmp[r * td:(r + 1) * td]
        pgs.append(pgrp)
    pg_all = jnp.concatenate(pgs, axis=0)
    imp = _mm(pg_all, mimp_ref[...], lax.Precision.HIGHEST)
    qpos_sel = past_len + _iota((NSA_KV_HEADS * td, 1), 0) % td
    sel = _select_blocks_rows(imp, qpos_sel, n_real)
    mask_ref[...] = _mm(sel.astype(BF16), exp_ref[...]).astype(BF16)


def _nsa_dec1(page_table, q, pe, wlo, whi, w2, mimp, expand, cache_chunks, e, n_p, bd, td, ns_pad):
    n_pages = page_table.shape[1]
    past_len = n_pages * PAGE
    off = n_p // td
    ncp = past_len // CMP_D
    full = lambda shape: pl.BlockSpec(shape, lambda b, pt: (0,) * len(shape))
    in_specs = [
        pl.BlockSpec((NSA_HEADS, td, HEAD_DIM), lambda b, pt: (0, off + b, 0)),
        full((2, CHUNK_W)), full((CHUNK_W, KV_W)), full((CHUNK_W, KV_W)), full((KV_W, KV_W)),
        full((ncp, ns_pad)), full((ns_pad, past_len)),
    ]
    in_specs += [pl.BlockSpec((None, None, KV_W, PAGE), functools.partial(
        lambda b, pt, i: (e, pt[b, i], 0, 0), i=i)) for i in range(n_pages)]
    grid_spec = pltpu.PrefetchScalarGridSpec(
        num_scalar_prefetch=1,
        grid=(bd,),
        in_specs=in_specs,
        out_specs=[pl.BlockSpec((td, D_MODEL), lambda b, pt: (b, 0)),
                   pl.BlockSpec((None, NSA_KV_HEADS * td, past_len), lambda b, pt: (b, 0, 0))],
        scratch_shapes=[pltpu.VMEM((2, past_len, LANES), F32)],
    )
    return pl.pallas_call(
        functools.partial(_nsa_dec1_kernel, n_pages=n_pages, td=td, past_len=past_len, ns_pad=ns_pad),
        grid_spec=grid_spec,
        out_shape=[jax.ShapeDtypeStruct((bd * td, D_MODEL), F32),
                   jax.ShapeDtypeStruct((bd, NSA_KV_HEADS * td, past_len), BF16)],
        compiler_params=_cp(("parallel",)),
        name="nsa_decode_select",
    )(page_table, q, pe, wlo, whi, w2, mimp, expand, *([cache_chunks] * n_pages))


def _nsa_dec2_kernel(pt_ref, q_ref, gate_ref, ocmp_ref, mask_ref, slcn_ref, wnn_ref, state_ref, *rest,
                     n_pages, td, past_len):
    pages = rest[:n_pages]
    o_ref, nst_ref = rest[n_pages:]
    rq = NSA_HEADS // NSA_KV_HEADS
    rows = rq * td
    wbuf = state_ref.shape[1]

    def rows_of(lo, hi):
        return jnp.concatenate([pages[i][lo:hi, :].astype(BF16) for i in range(n_pages)], axis=1)

    qgs = [q_ref[rq * g:rq * (g + 1)].reshape(rows, HEAD_DIM).astype(BF16) for g in range(NSA_KV_HEADS)]
    msk = mask_ref[...].astype(F32)

    gates = gate_ref[...]
    ocmp = ocmp_ref[...]
    slcn = slcn_ref[...].astype(BF16)
    wnn = wnn_ref[...]
    st = state_ref[...]
    eye = (_iota((KV_W, KV_W), 0) == _iota((KV_W, KV_W), 1)).astype(F32)
    place = (_iota((td, wbuf), 1) == _iota((td, wbuf), 0) + (wbuf - td)).astype(F32)
    wnn_t = _nt(eye, wnn, lax.Precision.HIGHEST)
    tail = _mm(wnn_t, place, lax.Precision.HIGHEST)
    nst_ref[...] = jnp.where(_iota((KV_W, wbuf), 1) < wbuf - td, pltpu.roll(st, wbuf - td, 1), tail)
    st16 = st.astype(BF16)
    wnn16 = wnn.astype(BF16)
    keep_new = _iota((rows, td), 1) <= _iota((rows, td), 0) % td
    dpos = (past_len + _iota((rows, wbuf), 0) % td) - (past_len - wbuf + _iota((rows, wbuf), 1))
    keep_st = (dpos >= 0) & (dpos < WINDOW)
    for g in range(NSA_KV_HEADS):
        ko, vo = HEAD_DIM * g, HEAD_DIM * (NSA_KV_HEADS + g)
        kt = rows_of(ko, ko + HEAD_DIM)
        vt = rows_of(vo, vo + HEAD_DIM)
        keep = jnp.broadcast_to((msk[td * g:td * (g + 1)] > 0.5)[None], (rq, td, past_len)).reshape(rows, past_len)
        s = jnp.where(keep, _mm(qgs[g], kt), NEG_INF)
        sn = jnp.where(keep_new, _nt(qgs[g], slcn[:, ko:ko + HEAD_DIM]), NEG_INF)
        o_slc = _joint_softmax(s, sn, lambda p: _nt(p, vt), lambda p: _mm(p, slcn[:, vo:vo + HEAD_DIM]))
        s1 = jnp.where(keep_st, _mm(qgs[g], st16[ko:ko + HEAD_DIM]), NEG_INF)
        s2 = jnp.where(keep_new, _nt(qgs[g], wnn16[:, ko:ko + HEAD_DIM]), NEG_INF)
        o_win = _joint_softmax(s1, s2, lambda p: _nt(p, st16[vo:vo + HEAD_DIM]),
                               lambda p: _mm(p, wnn16[:, vo:vo + HEAD_DIM]))
        for r in range(rq):
            h = rq * g + r
            rs = slice(r * td, (r + 1) * td)
            cols = slice(HEAD_DIM * h, HEAD_DIM * (h + 1))
            o_ref[:, cols] = (gates[:, h:h + 1] * ocmp[:, cols]
                              + gates[:, NSA_HEADS + h:NSA_HEADS + h + 1] * o_slc[rs]
                              + gates[:, 2 * NSA_HEADS + h:2 * NSA_HEADS + h + 1] * o_win[rs])


def _nsa_dec2(page_table, q, gates, ocmp, mask, slc32, wn32, state, cache_slc, e, n_p, bd, td):
    n_pages = page_table.shape[1]
    past_len = n_pages * PAGE
    off = n_p // td
    wbuf = state.shape[3]
    in_specs = [
        pl.BlockSpec((NSA_HEADS, td, HEAD_DIM), lambda b, pt: (0, off + b, 0)),
        pl.BlockSpec((td, N_GATES), lambda b, pt: (off + b, 0)),
        pl.BlockSpec((td, D_MODEL), lambda b, pt: (b, 0)),
        pl.BlockSpec((None, NSA_KV_HEADS * td, past_len), lambda b, pt: (b, 0, 0)),
        pl.BlockSpec((td, KV_W), lambda b, pt: (off + b, 0)),
        pl.BlockSpec((td, KV_W), lambda b, pt: (off + b, 0)),
        pl.BlockSpec((None, None, KV_W, wbuf), lambda b, pt: (e, b, 0, 0)),
    ]
    in_specs += [pl.BlockSpec((None, None, KV_W, PAGE), functools.partial(
        lambda b, pt, i: (e, pt[b, i], 0, 0), i=i)) for i in range(n_pages)]
    grid_spec = pltpu.PrefetchScalarGridSpec(
        num_scalar_prefetch=1,
        grid=(bd,),
        in_specs=in_specs,
        out_specs=[pl.BlockSpec((td, D_MODEL), lambda b, pt: (b, 0)),
                   pl.BlockSpec((None, KV_W, wbuf), lambda b, pt: (b, 0, 0))],
    )
    return pl.pallas_call(
        functools.partial(_nsa_dec2_kernel, n_pages=n_pages, td=td, past_len=past_len),
        grid_spec=grid_spec,
        out_shape=[jax.ShapeDtypeStruct((bd * td, D_MODEL), F32),
                   jax.ShapeDtypeStruct((bd, KV_W, wbuf), F32)],
        compiler_params=_cp(("parallel",)),
        name="nsa_decode_attend",
    )(page_table, q, gates, ocmp, mask, slc32, wn32, state, *([cache_slc] * n_pages))


def _prep_even(w_in, b_f, w_q_up, w_uk, w_uv):
    pad = jnp.zeros((D_MODEL, EV_COLS - 1192), F32)
    w = jnp.concatenate([w_in[:, 0:768], w_in[:, 776:1192], w_in[:, 768:776], pad], axis=1).astype(BF16)
    bf = jnp.zeros((1, LANES), F32).at[0, EV_FLG_LANE:EV_FLG_LANE + FOX_HEADS].set(b_f)
    wq = w_q_up.reshape(MLA_Q_LORA, MLA_HEADS, MLA_QK)
    wq = jnp.concatenate([wq[:, :, :MLA_NOPE].reshape(MLA_Q_LORA, -1),
                          wq[:, :, MLA_NOPE:].reshape(MLA_Q_LORA, -1)], axis=1).astype(BF16)
    wuk_h = jnp.transpose(w_uk, (1, 2, 0))
    eye = jnp.eye(MLA_HEADS, dtype=F32)
    wuk_bd = (eye[:, None, :, None] * wuk_h[:, :, None, :]).reshape(MLA_HEADS * MLA_NOPE, MLA_HEADS * MLA_KV_LORA)
    wuv = jnp.transpose(w_uv, (1, 0, 2)).astype(BF16)
    return w, bf, wq, wuk_bd.astype(BF16), wuv


def _prep_odd(w_in, b_gate, pe, w1, w2):
    qw = NSA_HEADS * HEAD_DIM
    seg = lambda n: w_in[:, qw + LANES * n:qw + LANES * (n + 1)]
    gcol = w_in[:, qw + 6 * LANES:qw + 6 * LANES + N_GATES]
    perm = np.array([h * 3 + c for c in range(3) for h in range(NSA_HEADS)])
    pad = jnp.zeros((D_MODEL, LANES - N_GATES), F32)
    w = jnp.concatenate([w_in[:, :qw], seg(0), seg(2), seg(4), seg(1), seg(3), seg(5), gcol[:, perm], pad],
                        axis=1).astype(BF16)
    bg = jnp.zeros((1, LANES), F32).at[0, 0:N_GATES].set(b_gate[perm])
    kv_of = (0, 0, 1, 1)
    pe_rows, w_halves = [], []
    eye4 = jnp.eye(4, dtype=F32)
    for half in range(2):
        ls = slice(half * CMP_D, (half + 1) * CMP_D)
        pe_rows.append(jnp.stack([pe[kv_of[j], ls, :] for j in range(4)], axis=1).reshape(CHUNK_W))
        w1r = jnp.stack([w1[kv_of[j]].reshape(CMP_L, HEAD_DIM, HEAD_DIM)[ls] for j in range(4)], axis=1)
        wexp = (w1r[:, :, :, None, :] * eye4[None, :, None, :, None]).reshape(CHUNK_W, KV_W)
        w_halves.append(wexp.astype(BF16))
    w2bd = (eye4[:, None, :, None] * jnp.stack([w2[kv_of[j]] for j in range(4)])[:, :, None, :]).reshape(KV_W, KV_W)
    return w, bg, jnp.stack(pe_rows), w_halves[0], w_halves[1], w2bd.astype(BF16)


def kernel(x_prompt, x_sample, cache_fox_kv, cache_fox_logf, cache_mla, cache_nsa_cmp, cache_nsa_slc,
           state_nsa_win, page_table, norm_w, final_norm, ffn_w_gate, ffn_w_up, ffn_w_down,
           ev_w_in, ev_b_f, mla_q_norm, mla_w_q_up, mla_kv_norm, mla_w_uk, mla_w_uv, ev_w_out,
           od_w_in, od_b_gate, nsa_cmp_pe, nsa_cmp_w1, nsa_cmp_w2, od_w_out):
    b, t, _ = x_prompt.shape
    bd, td, _ = x_sample.shape
    depth = norm_w.shape[0]
    n_pages = page_table.shape[1]
    past_len = n_pages * PAGE
    n_p, n_s = b * t, bd * td
    n = n_p + n_s
    wbuf = state_nsa_win.shape[2]

    tm = math.gcd(512, n_s)
    tf = FF_DIM // 2
    tm_ffn = 768 if n % 768 == 0 else tm
    tq = min(256, t)
    tk = min(512, t)
    ns_pad = -(-(-(-(past_len + td) // SEL_L)) // LANES) * LANES

    rope_mla = _rope_tables(t, past_len, td, tm, MLA_THETA, MLA_ROPE, MLA_ROPE)
    rope_nsa = _rope_tables(t, past_len, td, tm, ROPE_THETA, ROPE_DIM, HEAD_DIM)
    mimp_p = _imp_matrix(t // SEL_L, t // CMP_D)
    mimp_s = _imp_matrix(ns_pad, past_len // CMP_D).T
    expand_p = _expand_matrix(t // SEL_L, t)
    expand_s = _expand_matrix(ns_pad, past_len)
    eye_q = jnp.eye(tq, dtype=BF16)

    wg16, wu16, wd16 = ffn_w_gate.astype(BF16), ffn_w_up.astype(BF16), ffn_w_down.astype(BF16)
    tok_minor = lambda c: jnp.transpose(c, (0, 1, 3, 4, 5, 2)).reshape(c.shape[0], c.shape[1], KV_W, c.shape[2])
    cache_kv2 = tok_minor(cache_fox_kv)
    cache_lft = jnp.swapaxes(cache_fox_logf, 2, 3)
    cache_lat2 = jnp.swapaxes(cache_mla, 2, 3)
    cache_cmp2 = tok_minor(cache_nsa_cmp)
    cache_slc2 = tok_minor(cache_nsa_slc)
    state2 = tok_minor(state_nsa_win)

    h = jnp.concatenate([x_prompt.reshape(n_p, D_MODEL), x_sample.reshape(n_s, D_MODEL)], axis=0)
    row = lambda v: v.reshape(1, -1)
    kv_shape = (2, FOX_KV_HEADS, HEAD_DIM)
    outs = {k: [] for k in ("fkv_p", "fkv_s", "flf_p", "flf_s", "mla_p", "mla_s",
                            "cmp_p", "cmp_s", "slc_p", "slc_s", "win_p", "win_s")}
    for li in range(depth):
        e = li // 2
        h = _ffn(h, row(norm_w[li, 0]), wg16[li, 0], wu16[li, 0], wd16[li, 0], tm_ffn, tf)
        if li % 2 == 0:
            w, bf, wq, wuk_bd, wuv = _prep_even(ev_w_in[e], ev_b_f[e], mla_w_q_up[e], mla_w_uk[e], mla_w_uv[e])
            proj = functools.partial(_even_proj, h, row(norm_w[li, 1]), w, bf, row(mla_q_norm[e]), wq, wuk_bd,
                                     row(mla_kv_norm[e]), rope_mla, tm, t)
            qf, kv32, kv16, logf, qm, lat32, lat16 = proj(0, n_p, False)
            qf_s, kv32_s, _, logf_s, qm_s, lat32_s, _ = proj(n_p // tm, n_s, True)
            csum, cst = _csum(logf, b, t, tq)
            o_p = _even_attn(qf, qm, csum, kv16, lat16, cst, wuv, b, t, tq, tk)
            o_s = _even_dec(page_table, qf_s, qm_s, kv32_s, lat32_s, logf_s, wuv, cache_kv2, cache_lft, cache_lat2,
                            e, 0, bd, td)
            h = _out_proj(h, o_p, o_s, ev_w_out[e].astype(BF16), tm)
            outs["fkv_p"].append(kv32.reshape((b, t) + kv_shape))
            outs["fkv_s"].append(kv32_s.reshape((bd, td) + kv_shape))
            outs["flf_p"].append(logf.reshape(b, t, FOX_HEADS))
            outs["flf_s"].append(logf_s.reshape(bd, td, FOX_HEADS))
            outs["mla_p"].append(lat32.reshape(b, t, MLA_LAT))
            outs["mla_s"].append(lat32_s.reshape(bd, td, MLA_LAT))
        else:
            w, bg, pe2, wlo, whi, w2bd = _prep_odd(od_w_in[e], od_b_gate[e], nsa_cmp_pe[e], nsa_cmp_w1[e],
                                                   nsa_cmp_w2[e])
            proj = functools.partial(_odd_proj, h, row(norm_w[li, 1]), w, bg, rope_nsa, tm, t)
            q, cmp32, slc32, wn32, slc16, wn16, gates = proj(0, n_p, False)
            q_s, cmp32_s, slc32_s, wn32_s, _, _, gates_s = proj(n_p // tm, n_s, True)
            kvc = _compress_prompt(cmp32.reshape(n_p // CMP_D, CHUNK_W), pe2, wlo, whi, w2bd, b, t // CMP_D)
            o_p = _nsa_prompt(q, gates, kvc, slc16, wn16, mimp_p, eye_q, expand_p, b, t, tq, tk)
            ocmp, mask = _nsa_dec1(page_table, q_s, pe2, wlo, whi, w2bd, mimp_s, expand_s, cache_cmp2,
                                   e, 0, bd, td, ns_pad)
            o_s, nst = _nsa_dec2(page_table, q_s, gates_s, ocmp, mask, slc32_s, wn32_s, state2, cache_slc2,
                                 e, 0, bd, td)
            h = _out_proj(h, o_p, o_s, od_w_out[e].astype(BF16), tm)
            wlen = min(WINDOW, t)
            outs["cmp_p"].append(cmp32.reshape((b, t) + kv_shape))
            outs["cmp_s"].append(cmp32_s.reshape((bd, td) + kv_shape))
            outs["slc_p"].append(slc32.reshape((b, t) + kv_shape))
            outs["slc_s"].append(slc32_s.reshape((bd, td) + kv_shape))
            outs["win_p"].append(wn32.reshape((b, t) + kv_shape)[:, t - wlen:])
            outs["win_s"].append(jnp.transpose(nst.reshape((bd,) + kv_shape + (wbuf,)), (0, 4, 1, 2, 3)))
        h = _ffn(h, row(norm_w[li, 2]), wg16[li, 1], wu16[li, 1], wd16[li, 1], tm_ffn, tf)
    y_p = _final_norm(h, row(final_norm), tm, 0, n_p)
    y_s = _final_norm(h, row(final_norm), tm, n_p // tm, n_s)
    st = lambda k: jnp.stack(outs[k])
    return (y_p.reshape(b, t, D_MODEL), y_s.reshape(bd, td, D_MODEL),
            st("fkv_p"), st("fkv_s"), st("flf_p"), st("flf_s"), st("mla_p"), st("mla_s"),
            st("cmp_p"), st("cmp_s"), st("slc_p"), st("slc_s"), st("win_p"), st("win_s"))
```
